```python
import math
import jax, jax.numpy as jnp
from jax import lax
import numpy as np

D_MODEL = 1024
BATCH = 2
SEQ = 16384
DEPTH = 4

MEM_LEN = 256
HEAD_DIM = 64
DIFF_HEADS = 4
DIFF_QK = 32
DIFF_V = 2 * DIFF_QK
MLA_HEADS = 4
MLA_Q_RANK = 256
MLA_KV_RANK = 128
MLA_NOPE = 64
MLA_ROPE = 32
MLA_V = 64
SWA_Q_HEADS = 8
SWA_KV_HEADS = 2
SWA_GROUP = SWA_Q_HEADS // SWA_KV_HEADS
SWA_WINDOW = 128
D_MIX = DIFF_HEADS * DIFF_V + MLA_HEADS * MLA_V + SWA_Q_HEADS * HEAD_DIM
IN_SIZES = [DIFF_HEADS * 2 * DIFF_QK, DIFF_HEADS * 2 * DIFF_QK, DIFF_HEADS * DIFF_V,
            MLA_Q_RANK, MLA_KV_RANK, MLA_ROPE,
            SWA_Q_HEADS * HEAD_DIM, SWA_KV_HEADS * HEAD_DIM, SWA_KV_HEADS * HEAD_DIM]
D_IN = sum(IN_SIZES)
IN_SPLITS = [int(v) for v in np.cumsum(IN_SIZES)[:-1]]
X_HEADS = 4
X_HEAD_DIM = D_MODEL // X_HEADS
D_FF = -(-8 * D_MODEL // (3 * 256)) * 256

N_ALIBI = DIFF_HEADS + SWA_Q_HEADS
QBLK = 128
ROPE_THETA = 10000.0
EPS = 1e-6

kernel_name = "hymba_style_hybrid_encoder"


def rmsnorm(x, g):
    xf = x.astype(jnp.float32)
    y = xf * lax.rsqrt(jnp.mean(xf * xf, axis=-1, keepdims=True) + EPS)
    return (y * g.astype(jnp.float32)).astype(x.dtype)


def rope(x, pos):
    half = x.shape[-1] // 2
    inv = ROPE_THETA ** (-jnp.arange(half, dtype=jnp.float32) / half)
    ang = pos.astype(jnp.float32)[..., None] * inv
    ang = ang.reshape(ang.shape[:2] + (1,) * (x.ndim - 3) + (half,))
    cos, sin = jnp.cos(ang), jnp.sin(ang)
    xf = x.astype(jnp.float32)
    x1, x2 = xf[..., :half], xf[..., half:]
    return jnp.concatenate([x1 * cos - x2 * sin, x1 * sin + x2 * cos], axis=-1).astype(x.dtype)


def alibi_slopes():
    return 2.0 ** (-8.0 * jnp.arange(1, N_ALIBI + 1, dtype=jnp.float32) / N_ALIBI)


def diff_attention(q, k, v, pos, slopes, lam):
    B, S, H, _ = q.shape
    nblk = S // QBLK
    scale = 1.0 / math.sqrt(DIFF_QK)
    qb = q.reshape(B, nblk, QBLK, H, 2, DIFF_QK).swapaxes(0, 1)
    pb = pos.reshape(B, nblk, QBLK).swapaxes(0, 1)
    k2 = k.reshape(B, S, H, 2, DIFF_QK)

    def one_block(args):
        qi, pi = args
        s = jnp.einsum('bqhmd,bshmd->bmhqs', qi, k2).astype(jnp.float32) * scale
        dist = jnp.abs(pi[:, :, None] - pos[:, None, :]).astype(jnp.float32)
        s = s - slopes[None, None, :, None, None] * dist[:, None, None]
        p = jax.nn.softmax(s, axis=-1)
        a = p[:, 0] - lam * p[:, 1]
        return jnp.einsum('bhqs,bshd->bqhd', a.astype(v.dtype), v)

    out = lax.map(one_block, (qb, pb))
    return out.swapaxes(0, 1).reshape(B, S, H, DIFF_V)


def mla_attention(q, k, v):
    B, S, H, D = q.shape
    nblk = S // QBLK
    scale = 1.0 / math.sqrt(D)
    qb = q.reshape(B, nblk, QBLK, H, D).swapaxes(0, 1)

    def one_block(qi):
        s = jnp.einsum('bqhd,bshd->bhqs', qi, k).astype(jnp.float32) * scale
        p = jax.nn.softmax(s, axis=-1)
        return jnp.einsum('bhqs,bshd->bqhd', p.astype(v.dtype), v)

    out = lax.map(one_block, qb)
    return out.swapaxes(0, 1).reshape(B, S, H * MLA_V)


def swa_attention(q, k, v, pos, slopes, sinks):
    B, S = q.shape[:2]
    W = SWA_WINDOW
    nblk = S // W
    scale = 1.0 / math.sqrt(HEAD_DIM)

    def band(t):
        pad = [(0, 0), (W, W)] + [(0, 0)] * (t.ndim - 2)
        tp = jnp.pad(t, pad).reshape((t.shape[0], nblk + 2, W) + t.shape[2:])
        return jnp.concatenate([tp[:, :-2], tp[:, 1:-1], tp[:, 2:]], axis=2)

    kb, vb, pkb = band(k), band(v), band(pos)
    valid = band(jnp.ones((1, S), dtype=bool))
    qg = q.reshape(B, nblk, W, SWA_KV_HEADS, SWA_GROUP, HEAD_DIM)
    s = jnp.einsum('bnqkgd,bnskd->bnkgqs', qg, kb).astype(jnp.float32) * scale
    pq = pos.reshape(B, nblk, W)
    dist = jnp.abs(pq[..., :, None] - pkb[..., None, :]).astype(jnp.float32)
    s = s - slopes.reshape(SWA_KV_HEADS, SWA_GROUP)[:, :, None, None] * dist[:, :, None, None]
    offs = jnp.arange(3 * W)[None, :] - W - jnp.arange(W)[:, None]
    mask = (jnp.abs(offs) <= W)[None, None] & valid[:, :, None, :]
    s = jnp.where(mask[:, :, None, None], s, jnp.finfo(jnp.float32).min)
    sink = sinks.astype(jnp.float32).reshape(SWA_KV_HEADS, SWA_GROUP)[:, :, None, None]
    m = jnp.maximum(jnp.max(s, axis=-1, keepdims=True), sink)
    e = jnp.exp(s - m)
    p = e / (jnp.sum(e, axis=-1, keepdims=True) + jnp.exp(sink - m))
    o = jnp.einsum('bnkgqs,bnskd->bnqkgd', p.astype(v.dtype), vb)
    return o.reshape(B, S, SWA_Q_HEADS * HEAD_DIM)


def cross_attention(h, mem_n, w_q, w_kv, w_o):
    B, S, _ = h.shape
    M = mem_n.shape[1]
    q = (h @ w_q).reshape(B, S, X_HEADS, X_HEAD_DIM)
    k, v = jnp.split(mem_n @ w_kv, 2, axis=-1)
    k = k.reshape(B, M, X_HEADS, X_HEAD_DIM)
    v = v.reshape(B, M, X_HEADS, X_HEAD_DIM)
    s = jnp.einsum('bqhd,bmhd->bhqm', q, k).astype(jnp.float32) / math.sqrt(X_HEAD_DIM)
    p = jax.nn.softmax(s, axis=-1)
    o = jnp.einsum('bhqm,bmhd->bqhd', p.astype(v.dtype), v).reshape(B, S, D_MODEL)
    return o @ w_o


def setup_inputs(seed: int = 0) -> dict:
    key = jax.random.key(seed)
    ks = jax.random.split(key, 32)
    f32 = jnp.float32

    def w(k, shape, fan_in):
        return jax.random.normal(k, shape, f32) * fan_in ** -0.5

    def gain(k, shape):
        return 1.0 + 0.02 * jax.random.normal(k, shape, f32)

    offset = jax.random.randint(ks[2], (BATCH, 1), 0, 1024, dtype=jnp.int32)
    positions = offset + jnp.arange(SEQ, dtype=jnp.int32)[None, :]
    return {
        "x": jax.random.normal(ks[0], (BATCH, SEQ, D_MODEL), f32),
        "mem": jax.random.normal(ks[1], (BATCH, MEM_LEN, D_MODEL), f32),
        "positions": positions,
        "g_mix_pre": gain(ks[3], (DEPTH, D_MODEL)),
        "g_mix_post": gain(ks[4], (DEPTH, D_MODEL)),
        "w_in": w(ks[5], (DEPTH, D_MODEL, D_IN), D_MODEL),
        "diff_lambda": 0.1 * jax.random.normal(ks[6], (DEPTH, 4, DIFF_QK), f32),
        "diff_head_g": gain(ks[7], (DEPTH, DIFF_V)),
        "mla_q_norm_g": gain(ks[8], (DEPTH, MLA_Q_RANK)),
        "mla_w_q_up": w(ks[9], (DEPTH, MLA_Q_RANK, MLA_HEADS * (MLA_NOPE + MLA_ROPE)), MLA_Q_RANK),
        "mla_kv_norm_g": gain(ks[10], (DEPTH, MLA_KV_RANK)),
        "mla_w_kv_up": w(ks[11], (DEPTH, MLA_KV_RANK, MLA_HEADS * (MLA_NOPE + MLA_V)), MLA_KV_RANK),
        "swa_sinks": 0.5 * jax.random.normal(ks[12], (DEPTH, SWA_Q_HEADS), f32),
        "w_out": w(ks[13], (DEPTH, D_MIX, D_MODEL), D_MIX),
        "g_x_pre": gain(ks[14], (DEPTH, D_MODEL)),
        "g_x_mem": gain(ks[15], (DEPTH, D_MODEL)),
        "g_x_post": gain(ks[16], (DEPTH, D_MODEL)),
        "w_xq": w(ks[17], (DEPTH, D_MODEL, D_MODEL), D_MODEL),
        "w_xkv": w(ks[18], (DEPTH, D_MODEL, 2 * D_MODEL), D_MODEL),
        "w_xo": w(ks[19], (DEPTH, D_MODEL, D_MODEL), D_MODEL),
        "g_ffn_pre": gain(ks[20], (DEPTH, D_MODEL)),
        "g_ffn_post": gain(ks[21], (DEPTH, D_MODEL)),
        "w_ffn_in": w(ks[22], (DEPTH, D_MODEL, 2 * D_FF), D_MODEL),
        "w_ffn_out": w(ks[23], (DEPTH, D_FF, D_MODEL), D_FF),
    }


def reference(x, mem, positions, g_mix_pre, g_mix_post, w_in, diff_lambda, diff_head_g,
              mla_q_norm_g, mla_w_q_up, mla_kv_norm_g, mla_w_kv_up, swa_sinks, w_out,
              g_x_pre, g_x_mem, g_x_post, w_xq, w_xkv, w_xo,
              g_ffn_pre, g_ffn_post, w_ffn_in, w_ffn_out):
    B, S, _ = x.shape
    slopes = alibi_slopes()
    swa_slopes = slopes[:SWA_Q_HEADS]
    diff_slopes = slopes[SWA_Q_HEADS:]

    for l in range(DEPTH):
        h = rmsnorm(x, g_mix_pre[l])
        (a_q, a_k, a_v, b_cq, b_ckv, b_kr, c_q, c_k, c_v) = jnp.split(h @ w_in[l], IN_SPLITS, axis=-1)

        lam_init = 0.8 - 0.6 * math.exp(-0.3 * l)
        lp = diff_lambda[l].astype(jnp.float32)
        lam = jnp.exp(jnp.sum(lp[0] * lp[1])) - jnp.exp(jnp.sum(lp[2] * lp[3])) + lam_init
        o_a = diff_attention(a_q.reshape(B, S, DIFF_HEADS, 2 * DIFF_QK),
                             a_k.reshape(B, S, DIFF_HEADS, 2 * DIFF_QK),
                             a_v.reshape(B, S, DIFF_HEADS, DIFF_V),
                             positions, diff_slopes, lam)
        o_a = (rmsnorm(o_a, diff_head_g[l]) * (1.0 - lam_init)).reshape(B, S, DIFF_HEADS * DIFF_V)

        qb = (rmsnorm(b_cq, mla_q_norm_g[l]) @ mla_w_q_up[l]).reshape(B, S, MLA_HEADS, MLA_NOPE + MLA_ROPE)
        qb = jnp.concatenate([qb[..., :MLA_NOPE], rope(qb[..., MLA_NOPE:], positions)], axis=-1)
        kvb = (rmsnorm(b_ckv, mla_kv_norm_g[l]) @ mla_w_kv_up[l]).reshape(B, S, MLA_HEADS, MLA_NOPE + MLA_V)
        k_rope = jnp.broadcast_to(rope(b_kr, positions)[:, :, None, :], (B, S, MLA_HEADS, MLA_ROPE))
        kb = jnp.concatenate([kvb[..., :MLA_NOPE], k_rope], axis=-1)
        o_b = mla_attention(qb, kb, kvb[..., MLA_NOPE:])

        o_c = swa_attention(c_q.reshape(B, S, SWA_Q_HEADS, HEAD_DIM),
                            c_k.reshape(B, S, SWA_KV_HEADS, HEAD_DIM),
                            c_v.reshape(B, S, SWA_KV_HEADS, HEAD_DIM),
                            positions, swa_slopes, swa_sinks[l])

        mix = jnp.concatenate([o_a, o_b, o_c], axis=-1) @ w_out[l]
        x = x + rmsnorm(mix, g_mix_post[l])

        xo = cross_attention(rmsnorm(x, g_x_pre[l]), rmsnorm(mem, g_x_mem[l]), w_xq[l], w_xkv[l], w_xo[l])
        x = x + rmsnorm(xo, g_x_post[l])

        gate, up = jnp.split(rmsnorm(x, g_ffn_pre[l]) @ w_ffn_in[l], 2, axis=-1)
        f = (jax.nn.silu(gate) * up) @ w_ffn_out[l]
        x = x + rmsnorm(f, g_ffn_post[l])

    return x
```

```python
import functools
import math

import jax
import jax.numpy as jnp
import numpy as np
from jax import lax
from jax.experimental import pallas as pl
from jax.experimental.pallas import tpu as pltpu

F32 = jnp.float32
BF16 = jnp.bfloat16

D_MODEL = 1024
HEAD_DIM = 64
DIFF_HEADS = 4
DIFF_QK = 32
DIFF_V = 64
DIFF_MAPS = 2 * DIFF_HEADS
MLA_HEADS = 4
MLA_Q_RANK = 256
MLA_KV_RANK = 128
MLA_NOPE = 64
MLA_ROPE = 32
MLA_V = 64
SWA_Q_HEADS = 8
SWA_KV_HEADS = 2
SWA_GROUP = SWA_Q_HEADS // SWA_KV_HEADS
SWA_WINDOW = 128
X_HEADS = 4
X_HEAD_DIM = D_MODEL // X_HEADS
D_FF = 2816
N_ALIBI = DIFF_HEADS + SWA_Q_HEADS
ROPE_THETA = 10000.0
EPS = 1e-6
LOG2E = math.log2(math.e)
NEG_BIG = -1e30

LANE = 128
VMEM_LIMIT = 56 * 1024 * 1024

_OFF = np.cumsum([0, 256, 256, 256, MLA_Q_RANK, MLA_KV_RANK, MLA_ROPE, 512, 128, 128])
(O_AQ, O_AK, O_AV, O_BCQ, O_BCKV, O_BKR, O_CQ, O_CK, O_CV, O_END) = [int(v) for v in _OFF]

NAT_W = 896
TR_W = 1152


def _alibi_slopes():
    return [2.0 ** (-8.0 * i / N_ALIBI) for i in range(1, N_ALIBI + 1)]


SWA_SLOPES = _alibi_slopes()[:SWA_Q_HEADS]
DIFF_SLOPES = _alibi_slopes()[SWA_Q_HEADS:]


def _nt_dot(a, b):
    return lax.dot_general(a, b, (((1,), (1,)), ((), ())), preferred_element_type=F32)


def _dot(a, b):
    return jnp.dot(a, b, preferred_element_type=F32)


def _rms_rows(x, g):
    return x * lax.rsqrt(jnp.mean(x * x, axis=-1, keepdims=True) + EPS) * g


def _params(*sem):
    return pltpu.CompilerParams(dimension_semantics=sem, vmem_limit_bytes=VMEM_LIMIT)


def _proj_kernel(x_ref, g_ref, wnat_ref, wtr_ref, gq_ref, gkv_ref, wqa_ref, wqb_ref, wk_ref, wvt_ref,
                 pl_ref, tab_ref, cost_ref, sint_ref,
                 qtd_ref, kd_ref, vtd_ref, qtm_ref, km_ref, vtm_ref, qts_ref, ks_ref, vts_ref):
    h = _rms_rows(x_ref[0], g_ref[...]).astype(BF16)
    nat = _dot(h, wnat_ref[...])
    tr = _nt_dot(wtr_ref[...], h)

    c_diff = LOG2E / math.sqrt(DIFF_QK)
    c_mla = LOG2E / math.sqrt(MLA_NOPE + MLA_ROPE)
    c_swa = LOG2E / math.sqrt(HEAD_DIM)

    qtd_ref[0] = (tr[0:256] * c_diff).astype(BF16)
    vtd_ref[0] = tr[256:512].astype(BF16)
    qts_ref[0] = (tr[512:1024] * c_swa).astype(BF16)
    vts_ref[0] = tr[1024:1152].astype(BF16)
    kd_ref[0] = nat[:, 0:256].astype(BF16)
    ks_ref[0] = nat[:, 768:896].astype(BF16)

    cq_n = _rms_rows(nat[:, 256:512], gq_ref[...]).astype(BF16)
    ckv_n = _rms_rows(nat[:, 512:640], gkv_ref[...]).astype(BF16)

    u = nat[:, 640:768] * tab_ref[0]
    kr = (u + pltpu.roll(u, 96, 1)).astype(BF16)
    km_ref[0] = (_dot(ckv_n, wk_ref[...]) + _dot(kr, pl_ref[...])).astype(BF16)
    vtm_ref[0] = _nt_dot(wvt_ref[...], ckv_n).astype(BF16)

    qa = _nt_dot(wqa_ref[...], cq_n)
    qb = _nt_dot(wqb_ref[...], cq_n)
    cos_t = cost_ref[0]
    sin_t = sint_ref[0]
    for hd in range(MLA_HEADS):
        r0 = hd * LANE
        qtm_ref[0, r0:r0 + 64, :] = (qa[r0:r0 + 64] * c_mla).astype(BF16)
        rope = qa[r0 + 64:r0 + 96] * cos_t + qb[hd * 32:hd * 32 + 32] * sin_t
        qtm_ref[0, r0 + 64:r0 + 96, :] = (rope * c_mla).astype(BF16)
        qtm_ref[0, r0 + 96:r0 + 128, :] = jnp.zeros((32, rope.shape[1]), BF16)


def _proj_call(x, g, w, tabs, ts):
    B, S, _ = x.shape
    row = lambda n: pl.BlockSpec((1, ts, n), lambda b, i: (b, i, 0))
    col = lambda n: pl.BlockSpec((1, n, ts), lambda b, i: (b, 0, i))
    full = lambda a: pl.BlockSpec(a.shape, lambda b, i: (0,) * a.ndim)
    consts = [g, w["wnat"], w["wtr"], w["gq"], w["gkv"], w["wqa"], w["wqb"], w["wk"], w["wvt"], w["place"]]
    out_shape = [
        jax.ShapeDtypeStruct((B, 256, S), BF16), jax.ShapeDtypeStruct((B, S, 256), BF16),
        jax.ShapeDtypeStruct((B, 256, S), BF16), jax.ShapeDtypeStruct((B, 512, S), BF16),
        jax.ShapeDtypeStruct((B, S, 512), BF16), jax.ShapeDtypeStruct((B, 256, S), BF16),
        jax.ShapeDtypeStruct((B, 512, S), BF16), jax.ShapeDtypeStruct((B, S, 128), BF16),
        jax.ShapeDtypeStruct((B, 128, S), BF16),
    ]
    out_specs = [col(256), row(256), col(256), col(512), row(512), col(256), col(512), row(128), col(128)]
    return pl.pallas_call(
        _proj_kernel,
        grid=(B, S // ts),
        in_specs=[row(D_MODEL)] + [full(a) for a in consts] + [row(128), col(32), col(32)],
        out_specs=out_specs,
        out_shape=out_shape,
        compiler_params=_params("parallel", "parallel"),
        name="in_proj",
    )(x, *consts, tabs["tab"], tabs["cos_t"], tabs["sin_t"])


def _softmax_step(s, vt, m_ref, l_ref, acc_ref, idx):
    m_old = m_ref[idx:idx + 1, :]
    m_new = jnp.maximum(m_old, jnp.max(s, axis=0, keepdims=True))
    alpha = jnp.exp2(m_old - m_new)
    p = jnp.exp2(s - m_new)
    l_ref[idx:idx + 1, :] = alpha * l_ref[idx:idx + 1, :] + jnp.sum(p, axis=0, keepdims=True)
    acc_ref[idx] = alpha * acc_ref[idx] + _dot(vt, p.astype(BF16))
    m_ref[idx:idx + 1, :] = m_new


def _dense_kernel(qtd_ref, kd_ref, vtd_ref, qtm_ref, km_ref, vtm_ref, posq_ref, posk_ref,
                  lam_ref, hg_ref, linit_ref, od_ref, om_ref,
                  wq_ref, m_ref, l_ref, acc_ref):
    ki = pl.program_id(2)
    tq = qtd_ref.shape[2]

    @pl.when(ki == 0)
    def _init():
        m_ref[...] = jnp.full(m_ref.shape, NEG_BIG, F32)
        l_ref[...] = jnp.zeros(l_ref.shape, F32)
        acc_ref[...] = jnp.zeros(acc_ref.shape, F32)
        rows = lax.broadcasted_iota(jnp.int32, (LANE, tq), 0)
        for mp in range(DIFF_MAPS):
            slab = qtd_ref[0, (mp // 4) * LANE:(mp // 4 + 1) * LANE, :].astype(F32)
            keep = (rows // DIFF_QK) == (mp % 4)
            wq_ref[mp] = jnp.where(keep, slab, 0.0).astype(BF16)

    dist = jnp.abs(posk_ref[0] - posq_ref[0]).astype(F32)
    for hd in range(DIFF_HEADS):
        bias = dist * (DIFF_SLOPES[hd] * LOG2E)
        vt = vtd_ref[0, hd * DIFF_V:(hd + 1) * DIFF_V, :]
        for j in range(2):
            mp = 2 * hd + j
            k_slab = kd_ref[0, :, (mp // 4) * LANE:(mp // 4 + 1) * LANE]
            s = _dot(k_slab, wq_ref[mp]) - bias
            _softmax_step(s, vt, m_ref, l_ref, acc_ref, mp)
    for hd in range(MLA_HEADS):
        s = _dot(km_ref[0, :, hd * LANE:(hd + 1) * LANE], qtm_ref[0, hd * LANE:(hd + 1) * LANE, :])
        _softmax_step(s, vtm_ref[0, hd * MLA_V:(hd + 1) * MLA_V, :], m_ref, l_ref, acc_ref, DIFF_MAPS + hd)

    @pl.when(ki == pl.num_programs(2) - 1)
    def _finish():
        lp = lam_ref[...]
        lam_init = linit_ref[...]
        lam = (jnp.exp(jnp.sum(lp[0:1] * lp[1:2], axis=-1, keepdims=True))
               - jnp.exp(jnp.sum(lp[2:3] * lp[3:4], axis=-1, keepdims=True)) + lam_init)
        outs = []
        for hd in range(DIFF_HEADS):
            o = (acc_ref[2 * hd] / l_ref[2 * hd:2 * hd + 1, :]
                 - lam * (acc_ref[2 * hd + 1] / l_ref[2 * hd + 1:2 * hd + 2, :]))
            o = o * lax.rsqrt(jnp.mean(o * o, axis=0, keepdims=True) + EPS) * hg_ref[...]
            outs.append(o * (1.0 - lam_init))
        od_ref[0] = jnp.concatenate(outs, axis=0).T.astype(BF16)
        outs = [acc_ref[DIFF_MAPS + hd] / l_ref[DIFF_MAPS + hd:DIFF_MAPS + hd + 1, :]
                for hd in range(MLA_HEADS)]
        om_ref[0] = jnp.concatenate(outs, axis=0).T.astype(BF16)


def _dense_call(p, posq, posk, lam, hg, linit, tq, tk):
    B, _, S = p["qtd"].shape
    qcol = lambda n: pl.BlockSpec((1, n, tq), lambda b, i, k: (b, 0, i))
    krow = lambda n: pl.BlockSpec((1, tk, n), lambda b, i, k: (b, k, 0))
    kcol = lambda n: pl.BlockSpec((1, n, tk), lambda b, i, k: (b, 0, k))
    full = lambda a: pl.BlockSpec(a.shape, lambda b, i, k: (0,) * a.ndim)
    n_maps = DIFF_MAPS + MLA_HEADS
    return pl.pallas_call(
        _dense_kernel,
        grid=(B, S // tq, S // tk),
        in_specs=[qcol(256), krow(256), kcol(256), qcol(512), krow(512), kcol(256),
                  qcol(1), krow(1), full(lam), full(hg), full(linit)],
        out_specs=[pl.BlockSpec((1, tq, 256), lambda b, i, k: (b, i, 0))] * 2,
        out_shape=[jax.ShapeDtypeStruct((B, S, 256), BF16)] * 2,
        scratch_shapes=[pltpu.VMEM((DIFF_MAPS, LANE, tq), BF16),
                        pltpu.VMEM((16, tq), F32), pltpu.VMEM((16, tq), F32),
                        pltpu.VMEM((n_maps, 64, tq), F32)],
        compiler_params=_params("parallel", "parallel", "arbitrary"),
        name="dense_attn",
    )(p["qtd"], p["kd"], p["vtd"], p["qtm"], p["km"], p["vtm"], posq, posk, lam, hg, linit)


def _swa_kernel(qt_ref, kc_ref, kp_ref, kn_ref, vc_ref, vp_ref, vn_ref,
                posq_ref, pkc_ref, pkp_ref, pkn_ref, sink_ref, o_ref):
    i = pl.program_id(1)
    last = pl.num_programs(1) - 1
    tq = qt_ref.shape[2]
    w = SWA_WINDOW
    posq = posq_ref[0]
    qidx = lax.broadcasted_iota(jnp.int32, (1, tq), 1)
    segs = [(kp_ref, pkp_ref, vp_ref, -w, i > 0), (kc_ref, pkc_ref, vc_ref, 0, None),
            (kn_ref, pkn_ref, vn_ref, tq, i < last)]
    geo = []
    for k_ref, pk_ref, _, start, valid in segs:
        n = k_ref.shape[1]
        kidx = lax.broadcasted_iota(jnp.int32, (n, 1), 0) + start
        if valid is not None:
            kidx = kidx + jnp.where(valid, 0, 4 * (tq + w))
        ok = jnp.abs(kidx - qidx) <= w
        geo.append((jnp.abs(pk_ref[0] - posq).astype(F32), ok))
    zeros = jnp.zeros((HEAD_DIM, tq), BF16)
    outs = []
    for hq in range(SWA_Q_HEADS):
        kv = hq // SWA_GROUP
        q_h = qt_ref[0, hq * HEAD_DIM:(hq + 1) * HEAD_DIM, :]
        wq = jnp.concatenate([q_h, zeros] if kv == 0 else [zeros, q_h], axis=0)
        sink = sink_ref[hq:hq + 1, :] * LOG2E
        ss = []
        m = jnp.broadcast_to(sink, (1, tq))
        for (k_ref, _, _, _, _), (dist, ok) in zip(segs, geo):
            s = _dot(k_ref[0], wq) - dist * (SWA_SLOPES[hq] * LOG2E)
            s = jnp.where(ok, s, NEG_BIG)
            m = jnp.maximum(m, jnp.max(s, axis=0, keepdims=True))
            ss.append(s)
        den = jnp.exp2(sink - m)
        acc = jnp.zeros((HEAD_DIM, tq), F32)
        for s, (_, _, v_ref, _, _) in zip(ss, segs):
            e = jnp.exp2(s - m)
            den = den + jnp.sum(e, axis=0, keepdims=True)
            acc = acc + _dot(v_ref[0, kv * HEAD_DIM:(kv + 1) * HEAD_DIM, :], e.astype(BF16))
        outs.append(acc / den)
    o_ref[0] = jnp.concatenate(outs, axis=0).T.astype(BF16)


def _swa_call(p, posq, posk, sinks, tq):
    B, _, S = p["qts"].shape
    w = SWA_WINDOW
    r = tq // w
    nblk = S // w
    prev = lambda b, i: (b, jnp.maximum(i * r - 1, 0), 0)
    nxt = lambda b, i: (b, jnp.minimum((i + 1) * r, nblk - 1), 0)
    prev_t = lambda b, i: (b, 0, jnp.maximum(i * r - 1, 0))
    nxt_t = lambda b, i: (b, 0, jnp.minimum((i + 1) * r, nblk - 1))
    cur = lambda b, i: (b, i, 0)
    cur_t = lambda b, i: (b, 0, i)
    return pl.pallas_call(
        _swa_kernel,
        grid=(B, S // tq),
        in_specs=[pl.BlockSpec((1, 512, tq), cur_t),
                  pl.BlockSpec((1, tq, 128), cur), pl.BlockSpec((1, w, 128), prev), pl.BlockSpec((1, w, 128), nxt),
                  pl.BlockSpec((1, 128, tq), cur_t), pl.BlockSpec((1, 128, w), prev_t),
                  pl.BlockSpec((1, 128, w), nxt_t),
                  pl.BlockSpec((1, 1, tq), cur_t),
                  pl.BlockSpec((1, tq, 1), cur), pl.BlockSpec((1, w, 1), prev), pl.BlockSpec((1, w, 1), nxt),
                  pl.BlockSpec(sinks.shape, lambda b, i: (0, 0))],
        out_specs=pl.BlockSpec((1, tq, 512), cur),
        out_shape=jax.ShapeDtypeStruct((B, S, 512), BF16),
        compiler_params=_params("parallel", "parallel"),
        name="swa_attn",
    )(p["qts"], p["ks"], p["ks"], p["ks"], p["vts"], p["vts"], p["vts"],
      posq, posk, posk, posk, sinks)


def _memkv_kernel(mem_ref, g_ref, w_ref, kv_ref):
    mem_n = _rms_rows(mem_ref[0], g_ref[...]).astype(BF16)
    kv_ref[0] = _dot(mem_n, w_ref[...]).astype(BF16)


def _memkv_call(mem, g, w):
    B, M, _ = mem.shape
    return pl.pallas_call(
        _memkv_kernel,
        grid=(B,),
        in_specs=[pl.BlockSpec((1, M, D_MODEL), lambda b: (b, 0, 0)),
                  pl.BlockSpec(g.shape, lambda b: (0, 0)), pl.BlockSpec(w.shape, lambda b: (0, 0))],
        out_specs=pl.BlockSpec((1, M, 2 * D_MODEL), lambda b: (b, 0, 0)),
        out_shape=jax.ShapeDtypeStruct((B, M, 2 * D_MODEL), BF16),
        compiler_params=_params("parallel"),
        name="mem_kv",
    )(mem, g, w)


def _post_kernel(x_ref, od_ref, om_ref, os_ref, wout_ref, gmp_ref, gxp_ref, wxq_ref, kv_ref, wxo_ref,
                 gxo_ref, o_ref):
    mix = (_dot(od_ref[0], wout_ref[0:256, :]) + _dot(om_ref[0], wout_ref[256:512, :])
           + _dot(os_ref[0], wout_ref[512:1024, :]))
    x1 = x_ref[0] + _rms_rows(mix, gmp_ref[...])
    hq = _rms_rows(x1, gxp_ref[...]).astype(BF16)
    q = (_dot(hq, wxq_ref[...]) * (LOG2E / math.sqrt(X_HEAD_DIM))).astype(BF16)
    heads = []
    for hd in range(X_HEADS):
        c0 = hd * X_HEAD_DIM
        s = _nt_dot(q[:, c0:c0 + X_HEAD_DIM], kv_ref[0, :, c0:c0 + X_HEAD_DIM])
        e = jnp.exp2(s - jnp.max(s, axis=-1, keepdims=True))
        den = jnp.sum(e, axis=-1, keepdims=True)
        o = _dot(e.astype(BF16), kv_ref[0, :, D_MODEL + c0:D_MODEL + c0 + X_HEAD_DIM])
        heads.append((o / den).astype(BF16))
    xo = _dot(jnp.concatenate(heads, axis=-1), wxo_ref[...])
    o_ref[0] = x1 + _rms_rows(xo, gxo_ref[...])


def _post_call(x, od, om, os_, wout, gmp, gxp, wxq, kv, wxo, gxo, ts):
    B, S, _ = x.shape
    row = lambda n: pl.BlockSpec((1, ts, n), lambda b, i: (b, i, 0))
    full = lambda a: pl.BlockSpec(a.shape, lambda b, i: (0,) * a.ndim)
    kv_spec = pl.BlockSpec((1,) + kv.shape[1:], lambda b, i: (b, 0, 0))
    return pl.pallas_call(
        _post_kernel,
        grid=(B, S // ts),
        in_specs=[row(D_MODEL), row(256), row(256), row(512), full(wout), full(gmp), full(gxp), full(wxq),
                  kv_spec, full(wxo), full(gxo)],
        out_specs=row(D_MODEL),
        out_shape=jax.ShapeDtypeStruct(x.shape, F32),
        compiler_params=_params("parallel", "parallel"),
        name="mix_cross",
    )(x, od, om, os_, wout, gmp, gxp, wxq, kv, wxo, gxo)


def _ffn_kernel(x_ref, gpre_ref, win_ref, wout_ref, gpost_ref, o_ref):
    x = x_ref[0]
    h = _rms_rows(x, gpre_ref[...]).astype(BF16)
    gate = _dot(h, win_ref[:, 0:D_FF])
    up = _dot(h, win_ref[:, D_FF:2 * D_FF])
    f = _dot((gate * jax.nn.sigmoid(gate) * up).astype(BF16), wout_ref[...])
    o_ref[0] = x + _rms_rows(f, gpost_ref[...])


def _ffn_call(x, gpre, win, wout, gpost, ts):
    B, S, _ = x.shape
    row = pl.BlockSpec((1, ts, D_MODEL), lambda b, i: (b, i, 0))
    full = lambda a: pl.BlockSpec(a.shape, lambda b, i: (0,) * a.ndim)
    return pl.pallas_call(
        _ffn_kernel,
        grid=(B, S // ts),
        in_specs=[row, full(gpre), full(win), full(wout), full(gpost)],
        out_specs=row,
        out_shape=jax.ShapeDtypeStruct(x.shape, F32),
        compiler_params=_params("parallel", "parallel"),
        name="ffn",
    )(x, gpre, win, wout, gpost)


def _prep_weights(w_in, mla_q_norm_g, mla_w_q_up, mla_kv_norm_g, mla_w_kv_up):
    swap = np.concatenate([np.arange(16, 32), np.arange(0, 16)])
    kr = w_in[:, O_BKR:O_CQ]
    wnat = jnp.concatenate(
        [w_in[:, O_AK:O_AV], w_in[:, O_BCQ:O_BCKV], w_in[:, O_BCKV:O_BKR], kr, kr[:, swap],
         jnp.zeros((D_MODEL, 64), w_in.dtype), w_in[:, O_CK:O_CV]], axis=1)
    wtr = jnp.concatenate(
        [w_in[:, O_AQ:O_AK], w_in[:, O_AV:O_BCQ], w_in[:, O_CQ:O_CK], w_in[:, O_CV:O_END]], axis=1).T
    qup = mla_w_q_up.reshape(MLA_Q_RANK, MLA_HEADS, MLA_NOPE + MLA_ROPE)
    wqa = jnp.concatenate([qup, jnp.zeros((MLA_Q_RANK, MLA_HEADS, 32), qup.dtype)], axis=2)
    wqa = wqa.reshape(MLA_Q_RANK, MLA_HEADS * LANE).T
    wqb = qup[:, :, MLA_NOPE:][:, :, swap].reshape(MLA_Q_RANK, MLA_HEADS * MLA_ROPE).T
    kvup = mla_w_kv_up.reshape(MLA_KV_RANK, MLA_HEADS, MLA_NOPE + MLA_V)
    wk = jnp.concatenate([kvup[:, :, :MLA_NOPE], jnp.zeros((MLA_KV_RANK, MLA_HEADS, 64), kvup.dtype)], axis=2)
    wk = wk.reshape(MLA_KV_RANK, MLA_HEADS * LANE)
    wvt = kvup[:, :, MLA_NOPE:].reshape(MLA_KV_RANK, MLA_HEADS * MLA_V).T
    place = np.zeros((LANE, MLA_HEADS * LANE), np.float32)
    for hd in range(MLA_HEADS):
        place[np.arange(32), hd * LANE + MLA_NOPE + np.arange(32)] = 1.0
    return {
        "wnat": wnat.astype(BF16), "wtr": wtr.astype(BF16),
        "gq": mla_q_norm_g.reshape(1, -1), "gkv": mla_kv_norm_g.reshape(1, -1),
        "wqa": wqa.astype(BF16), "wqb": wqb.astype(BF16), "wk": wk.astype(BF16), "wvt": wvt.astype(BF16),
        "place": jnp.asarray(place, BF16),
    }


def _rope_tables(positions):
    half = MLA_ROPE // 2
    inv = ROPE_THETA ** (-jnp.arange(half, dtype=F32) / half)
    ang = positions.astype(F32)[..., None] * inv
    cos, sin = jnp.cos(ang), jnp.sin(ang)
    cos2 = jnp.concatenate([cos, cos], axis=-1)
    sin2 = jnp.concatenate([-sin, sin], axis=-1)
    tab = jnp.concatenate([cos2, sin2, jnp.zeros(cos2.shape[:2] + (64,), F32)], axis=-1)
    return {"tab": tab, "cos_t": cos2.swapaxes(1, 2), "sin_t": sin2.swapaxes(1, 2)}


def _tile(n, want):
    t = min(n, want)
    assert n % t == 0, (n, t)
    return t


def kernel(x, mem, positions, g_mix_pre, g_mix_post, w_in, diff_lambda, diff_head_g, mla_q_norm_g, mla_w_q_up,
           mla_kv_norm_g, mla_w_kv_up, swa_sinks, w_out, g_x_pre, g_x_mem, g_x_post, w_xq, w_xkv, w_xo,
           g_ffn_pre, g_ffn_post, w_ffn_in, w_ffn_out):
    B, S, _ = x.shape
    depth = w_in.shape[0]
    ts_proj, ts_post, ts_ffn = _tile(S, 512), _tile(S, 512), _tile(S, 256)
    tq, tk, tq_swa = _tile(S, 512), _tile(S, 512), _tile(S, 512)

    tabs = _rope_tables(positions)
    posq = positions.reshape(B, 1, S)
    posk = positions.reshape(B, S, 1)
    row = lambda v: v.reshape(1, -1)

    for l in range(depth):
        w = _prep_weights(w_in[l], mla_q_norm_g[l], mla_w_q_up[l], mla_kv_norm_g[l], mla_w_kv_up[l])
        names = ("qtd", "kd", "vtd", "qtm", "km", "vtm", "qts", "ks", "vts")
        p = dict(zip(names, _proj_call(x, row(g_mix_pre[l]), w, tabs, ts_proj)))
        linit = jnp.full((1, 1), 0.8 - 0.6 * math.exp(-0.3 * l), F32)
        od, om = _dense_call(p, posq, posk, diff_lambda[l], diff_head_g[l].reshape(-1, 1), linit, tq, tk)
        os_ = _swa_call(p, posq, posk, swa_sinks[l].reshape(-1, 1), tq_swa)
        kv = _memkv_call(mem, row(g_x_mem[l]), w_xkv[l].astype(BF16))
        x = _post_call(x, od, om, os_, w_out[l].astype(BF16), row(g_mix_post[l]), row(g_x_pre[l]),
                       w_xq[l].astype(BF16), kv, w_xo[l].astype(BF16), row(g_x_post[l]), ts_post)
        x = _ffn_call(x, row(g_ffn_pre[l]), w_ffn_in[l].astype(BF16), w_ffn_out[l].astype(BF16),
                      row(g_ffn_post[l]), ts_ffn)
    return x
```

```python
import functools
import math

import jax
import jax.numpy as jnp
import numpy as np
from jax import lax
from jax.experimental import pallas as pl
from jax.experimental.pallas import tpu as pltpu

F32 = jnp.float32
BF16 = jnp.bfloat16

D_MODEL = 1024
HEAD_DIM = 64
DIFF_HEADS = 4
DIFF_QK = 32
DIFF_V = 64
DIFF_MAPS = 2 * DIFF_HEADS
MLA_HEADS = 4
MLA_Q_RANK = 256
MLA_KV_RANK = 128
MLA_NOPE = 64
MLA_ROPE = 32
MLA_V = 64
SWA_Q_HEADS = 8
SWA_KV_HEADS = 2
SWA_GROUP = SWA_Q_HEADS // SWA_KV_HEADS
SWA_WINDOW = 128
X_HEADS = 4
X_HEAD_DIM = D_MODEL // X_HEADS
D_FF = 2816
N_ALIBI = DIFF_HEADS + SWA_Q_HEADS
ROPE_THETA = 10000.0
EPS = 1e-6
LOG2E = math.log2(math.e)
NEG_BIG = -1e30

LANE = 128
VMEM_LIMIT = 56 * 1024 * 1024

_OFF = np.cumsum([0, 256, 256, 256, MLA_Q_RANK, MLA_KV_RANK, MLA_ROPE, 512, 128, 128])
(O_AQ, O_AK, O_AV, O_BCQ, O_BCKV, O_BKR, O_CQ, O_CK, O_CV, O_END) = [int(v) for v in _OFF]

NAT_W = 896
TR_W = 1152


def _alibi_slopes():
    return [2.0 ** (-8.0 * i / N_ALIBI) for i in range(1, N_ALIBI + 1)]


SWA_SLOPES = _alibi_slopes()[:SWA_Q_HEADS]
DIFF_SLOPES = _alibi_slopes()[SWA_Q_HEADS:]


def _nt_dot(a, b):
    return lax.dot_general(a, b, (((1,), (1,)), ((), ())), preferred_element_type=F32)


def _dot(a, b):
    return jnp.dot(a, b, preferred_element_type=F32)


def _rms_rows(x, g):
    return x * lax.rsqrt(jnp.mean(x * x, axis=-1, keepdims=True) + EPS) * g


def _params(*sem, flags=None):
    return pltpu.CompilerParams(dimension_semantics=sem, vmem_limit_bytes=VMEM_LIMIT, flags=flags)


def _proj_kernel(x_ref, g_ref, wnat_ref, wtr_ref, gq_ref, gkv_ref, wqa_ref, wqb_ref, wk_ref, wvt_ref,
                 pl_ref, tab_ref, cost_ref, sint_ref,
                 qtd_ref, kd_ref, vtd_ref, qtm_ref, km_ref, vtm_ref, qts_ref, ks_ref, vts_ref):
    h = _rms_rows(x_ref[0], g_ref[...]).astype(BF16)
    nat = _dot(h, wnat_ref[...])
    tr = _nt_dot(wtr_ref[...], h)

    c_diff = LOG2E / math.sqrt(DIFF_QK)
    c_mla = LOG2E / math.sqrt(MLA_NOPE + MLA_ROPE)
    c_swa = LOG2E / math.sqrt(HEAD_DIM)

    qtd_ref[0] = (tr[0:256] * c_diff).astype(BF16)
    vtd_ref[0] = tr[256:512].astype(BF16)
    qts_ref[0] = (tr[512:1024] * c_swa).astype(BF16)
    vts_ref[0] = tr[1024:1152].astype(BF16)
    kd_ref[0] = nat[:, 0:256].astype(BF16)
    ks_ref[0] = nat[:, 768:896].astype(BF16)

    cq_n = _rms_rows(nat[:, 256:512], gq_ref[...]).astype(BF16)
    ckv_n = _rms_rows(nat[:, 512:640], gkv_ref[...]).astype(BF16)

    u = nat[:, 640:768] * tab_ref[0]
    kr = (u + pltpu.roll(u, 96, 1)).astype(BF16)
    km_ref[0] = (_dot(ckv_n, wk_ref[...]) + _dot(kr, pl_ref[...])).astype(BF16)
    vtm_ref[0] = _nt_dot(wvt_ref[...], ckv_n).astype(BF16)

    qa = _nt_dot(wqa_ref[...], cq_n)
    qb = _nt_dot(wqb_ref[...], cq_n)
    cos_t = cost_ref[0]
    sin_t = sint_ref[0]
    for hd in range(MLA_HEADS):
        r0 = hd * LANE
        qtm_ref[0, r0:r0 + 64, :] = (qa[r0:r0 + 64] * c_mla).astype(BF16)
        rope = qa[r0 + 64:r0 + 96] * cos_t + qb[hd * 32:hd * 32 + 32] * sin_t
        qtm_ref[0, r0 + 64:r0 + 96, :] = (rope * c_mla).astype(BF16)
        qtm_ref[0, r0 + 96:r0 + 128, :] = jnp.zeros((32, rope.shape[1]), BF16)


def _proj_call(x, g, w, tabs, ts):
    B, S, _ = x.shape
    row = lambda n: pl.BlockSpec((1, ts, n), lambda b, i: (b, i, 0))
    col = lambda n: pl.BlockSpec((1, n, ts), lambda b, i: (b, 0, i))
    full = lambda a: pl.BlockSpec(a.shape, lambda b, i: (0,) * a.ndim)
    consts = [g, w["wnat"], w["wtr"], w["gq"], w["gkv"], w["wqa"], w["wqb"], w["wk"], w["wvt"], w["place"]]
    out_shape = [
        jax.ShapeDtypeStruct((B, 256, S), BF16), jax.ShapeDtypeStruct((B, S, 256), BF16),
        jax.ShapeDtypeStruct((B, 256, S), BF16), jax.ShapeDtypeStruct((B, 512, S), BF16),
        jax.ShapeDtypeStruct((B, S, 512), BF16), jax.ShapeDtypeStruct((B, 256, S), BF16),
        jax.ShapeDtypeStruct((B, 512, S), BF16), jax.ShapeDtypeStruct((B, S, 128), BF16),
        jax.ShapeDtypeStruct((B, 128, S), BF16),
    ]
    out_specs = [col(256), row(256), col(256), col(512), row(512), col(256), col(512), row(128), col(128)]
    return pl.pallas_call(
        _proj_kernel,
        grid=(B, S // ts),
        in_specs=[row(D_MODEL)] + [full(a) for a in consts] + [row(128), col(32), col(32)],
        out_specs=out_specs,
        out_shape=out_shape,
        compiler_params=_params("parallel", "parallel"),
        name="in_proj",
    )(x, *consts, tabs["tab"], tabs["cos_t"], tabs["sin_t"])


STRIP = 256
PIPE_DEPTH = 5


def _with_ones(vt):
    return jnp.concatenate([vt, jnp.ones((16, vt.shape[1]), vt.dtype)], axis=0)


def _dense_kernel(qtd_ref, kd_ref, vtd_ref, qtm_ref, km_ref, vtm_ref, posq_ref, posk_ref,
                  lam_ref, hg_ref, linit_ref, od_ref, om_ref,
                  wq_ref, m_ref, acc_ref, s_ref):
    ki = pl.program_id(2)
    tq = qtd_ref.shape[2]
    n_slots = s_ref.shape[0]

    @pl.when(ki == 0)
    def _init():
        m_ref[...] = jnp.full(m_ref.shape, NEG_BIG, F32)
        acc_ref[...] = jnp.zeros(acc_ref.shape, F32)
        rows = lax.broadcasted_iota(jnp.int32, (LANE, tq), 0)
        for mp in range(DIFF_MAPS):
            slab = qtd_ref[0, (mp // 4) * LANE:(mp // 4 + 1) * LANE, :].astype(F32)
            keep = (rows // DIFF_QK) == (mp % 4)
            wq_ref[mp] = jnp.where(keep, slab, 0.0).astype(BF16)

    def logits(mp, lanes, bias):
        if mp < DIFF_MAPS:
            slab = (mp // 4) * LANE
            return _dot(kd_ref[0, :, slab:slab + LANE], wq_ref[mp, :, lanes]) - bias[mp // 2]
        hd = mp - DIFF_MAPS
        return _dot(km_ref[0, :, hd * LANE:(hd + 1) * LANE], qtm_ref[0, hd * LANE:(hd + 1) * LANE, lanes])

    def values_t(mp):
        if mp < DIFF_MAPS:
            return _with_ones(vtd_ref[0, (mp // 2) * DIFF_V:(mp // 2 + 1) * DIFF_V, :])
        hd = mp - DIFF_MAPS
        return _with_ones(vtm_ref[0, hd * MLA_V:(hd + 1) * MLA_V, :])

    def stage1(j, mp, lanes, bias):
        s = logits(mp, lanes, bias)
        s_ref[j % n_slots] = s
        m_old = m_ref[mp:mp + 1, lanes]
        m_new = jnp.maximum(m_old, jnp.max(s, axis=0, keepdims=True))
        m_ref[mp:mp + 1, lanes] = m_new
        return m_new, jnp.exp2(m_old - m_new)

    def stage2(j, mp, lanes, m_new, alpha):
        p = jnp.exp2((s_ref[j % n_slots] - m_new).astype(BF16))
        acc_ref[mp, :, lanes] = alpha * acc_ref[mp, :, lanes] + _dot(values_t(mp), p)

    pending = []
    j = 0
    for st in range(tq // STRIP):
        lanes = slice(st * STRIP, (st + 1) * STRIP)
        dist = jnp.abs(posk_ref[0] - posq_ref[0, :, lanes]).astype(F32)
        bias = [dist * (slope * LOG2E) for slope in DIFF_SLOPES]
        for mp in range(DIFF_MAPS + MLA_HEADS):
            pending.append((j, mp, lanes) + stage1(j, mp, lanes, bias))
            j += 1
            if len(pending) > PIPE_DEPTH:
                stage2(*pending.pop(0))
    for item in pending:
        stage2(*item)

    @pl.when(ki == pl.num_programs(2) - 1)
    def _finish():
        lp = lam_ref[...]
        lam_init = linit_ref[...]
        lam = (jnp.exp(jnp.sum(lp[0:1] * lp[1:2], axis=-1, keepdims=True))
               - jnp.exp(jnp.sum(lp[2:3] * lp[3:4], axis=-1, keepdims=True)) + lam_init)
        def normalised(idx):
            return acc_ref[idx, 0:64, :] / acc_ref[idx, 64:65, :]

        outs = []
        for hd in range(DIFF_HEADS):
            o = normalised(2 * hd) - lam * normalised(2 * hd + 1)
            o = o * lax.rsqrt(jnp.mean(o * o, axis=0, keepdims=True) + EPS) * hg_ref[...]
            outs.append(o * (1.0 - lam_init))
        od_ref[0] = jnp.concatenate(outs, axis=0).T.astype(BF16)
        outs = [normalised(DIFF_MAPS + hd) for hd in range(MLA_HEADS)]
        om_ref[0] = jnp.concatenate(outs, axis=0).T.astype(BF16)


def _dense_call(p, posq, posk, lam, hg, linit, tq, tk):
    B, _, S = p["qtd"].shape
    qcol = lambda n: pl.BlockSpec((1, n, tq), lambda b, i, k: (b, 0, i))
    krow = lambda n: pl.BlockSpec((1, tk, n), lambda b, i, k: (b, k, 0))
    kcol = lambda n: pl.BlockSpec((1, n, tk), lambda b, i, k: (b, 0, k))
    full = lambda a: pl.BlockSpec(a.shape, lambda b, i, k: (0,) * a.ndim)
    n_maps = DIFF_MAPS + MLA_HEADS
    return pl.pallas_call(
        _dense_kernel,
        grid=(B, S // tq, S // tk),
        in_specs=[qcol(256), krow(256), kcol(256), qcol(512), krow(512), kcol(256),
                  qcol(1), krow(1), full(lam), full(hg), full(linit)],
        out_specs=[pl.BlockSpec((1, tq, 256), lambda b, i, k: (b, i, 0))] * 2,
        out_shape=[jax.ShapeDtypeStruct((B, S, 256), BF16)] * 2,
        scratch_shapes=[pltpu.VMEM((DIFF_MAPS, LANE, tq), BF16),
                        pltpu.VMEM((16, tq), F32),
                        pltpu.VMEM((n_maps, 80, tq), F32),
                        pltpu.VMEM((PIPE_DEPTH + 2, tk, STRIP), F32)],
        compiler_params=_params("parallel", "parallel", "arbitrary"),
        name="dense_attn",
    )(p["qtd"], p["kd"], p["vtd"], p["qtm"], p["km"], p["vtm"], posq, posk, lam, hg, linit)


def _swa_kernel(qt_ref, kc_ref, kp_ref, kn_ref, vc_ref, vp_ref, vn_ref,
                posq_ref, pkc_ref, pkp_ref, pkn_ref, sink_ref, o_ref):
    i = pl.program_id(1)
    last = pl.num_programs(1) - 1
    tq = qt_ref.shape[2]
    w = SWA_WINDOW
    posq = posq_ref[0]
    qidx = lax.broadcasted_iota(jnp.int32, (1, tq), 1)
    segs = [(kp_ref, pkp_ref, vp_ref, -w, i > 0), (kc_ref, pkc_ref, vc_ref, 0, None),
            (kn_ref, pkn_ref, vn_ref, tq, i < last)]
    geo = []
    for k_ref, pk_ref, _, start, valid in segs:
        n = k_ref.shape[1]
        kidx = lax.broadcasted_iota(jnp.int32, (n, 1), 0) + start
        if valid is not None:
            kidx = kidx + jnp.where(valid, 0, 4 * (tq + w))
        ok = jnp.abs(kidx - qidx) <= w
        geo.append((jnp.abs(pk_ref[0] - posq).astype(F32), ok))
    zeros = jnp.zeros((HEAD_DIM, tq), BF16)
    outs = []
    for hq in range(SWA_Q_HEADS):
        kv = hq // SWA_GROUP
        q_h = qt_ref[0, hq * HEAD_DIM:(hq + 1) * HEAD_DIM, :]
        wq = jnp.concatenate([q_h, zeros] if kv == 0 else [zeros, q_h], axis=0)
        sink = sink_ref[hq:hq + 1, :] * LOG2E
        ss = []
        m = jnp.broadcast_to(sink, (1, tq))
        for (k_ref, _, _, _, _), (dist, ok) in zip(segs, geo):
            s = _dot(k_ref[0], wq) - dist * (SWA_SLOPES[hq] * LOG2E)
            s = jnp.where(ok, s, NEG_BIG)
            m = jnp.maximum(m, jnp.max(s, axis=0, keepdims=True))
            ss.append(s)
        den = jnp.exp2(sink - m)
        acc = jnp.zeros((HEAD_DIM, tq), F32)
        for s, (_, _, v_ref, _, _) in zip(ss, segs):
            e = jnp.exp2(s - m)
            den = den + jnp.sum(e, axis=0, keepdims=True)
            acc = acc + _dot(v_ref[0, kv * HEAD_DIM:(kv + 1) * HEAD_DIM, :], e.astype(BF16))
        outs.append(acc / den)
    o_ref[0] = jnp.concatenate(outs, axis=0).T.astype(BF16)


def _swa_call(p, posq, posk, sinks, tq):
    B, _, S = p["qts"].shape
    w = SWA_WINDOW
    r = tq // w
    nblk = S // w
    prev = lambda b, i: (b, jnp.maximum(i * r - 1, 0), 0)
    nxt = lambda b, i: (b, jnp.minimum((i + 1) * r, nblk - 1), 0)
    prev_t = lambda b, i: (b, 0, jnp.maximum(i * r - 1, 0))
    nxt_t = lambda b, i: (b, 0, jnp.minimum((i + 1) * r, nblk - 1))
    cur = lambda b, i: (b, i, 0)
    cur_t = lambda b, i: (b, 0, i)
    return pl.pallas_call(
        _swa_kernel,
        grid=(B, S // tq),
        in_specs=[pl.BlockSpec((1, 512, tq), cur_t),
                  pl.BlockSpec((1, tq, 128), cur), pl.BlockSpec((1, w, 128), prev), pl.BlockSpec((1, w, 128), nxt),
                  pl.BlockSpec((1, 128, tq), cur_t), pl.BlockSpec((1, 128, w), prev_t),
                  pl.BlockSpec((1, 128, w), nxt_t),
                  pl.BlockSpec((1, 1, tq), cur_t),
                  pl.BlockSpec((1, tq, 1), cur), pl.BlockSpec((1, w, 1), prev), pl.BlockSpec((1, w, 1), nxt),
                  pl.BlockSpec(sinks.shape, lambda b, i: (0, 0))],
        out_specs=pl.BlockSpec((1, tq, 512), cur),
        out_shape=jax.ShapeDtypeStruct((B, S, 512), BF16),
        compiler_params=_params("parallel", "parallel"),
        name="swa_attn",
    )(p["qts"], p["ks"], p["ks"], p["ks"], p["vts"], p["vts"], p["vts"],
      posq, posk, posk, posk, sinks)


def _memkv_kernel(mem_ref, g_ref, w_ref, kv_ref):
    mem_n = _rms_rows(mem_ref[0], g_ref[...]).astype(BF16)
    kv_ref[0] = _dot(mem_n, w_ref[...]).astype(BF16)


def _memkv_call(mem, g, w):
    B, M, _ = mem.shape
    return pl.pallas_call(
        _memkv_kernel,
        grid=(B,),
        in_specs=[pl.BlockSpec((1, M, D_MODEL), lambda b: (b, 0, 0)),
                  pl.BlockSpec(g.shape, lambda b: (0, 0)), pl.BlockSpec(w.shape, lambda b: (0, 0))],
        out_specs=pl.BlockSpec((1, M, 2 * D_MODEL), lambda b: (b, 0, 0)),
        out_shape=jax.ShapeDtypeStruct((B, M, 2 * D_MODEL), BF16),
        compiler_params=_params("parallel"),
        name="mem_kv",
    )(mem, g, w)


def _post_kernel(x_ref, od_ref, om_ref, os_ref, wout_ref, gmp_ref, gxp_ref, wxq_ref, kv_ref, wxo_ref,
                 gxo_ref, o_ref):
    mix = (_dot(od_ref[0], wout_ref[0:256, :]) + _dot(om_ref[0], wout_ref[256:512, :])
           + _dot(os_ref[0], wout_ref[512:1024, :]))
    x1 = x_ref[0] + _rms_rows(mix, gmp_ref[...])
    hq = _rms_rows(x1, gxp_ref[...]).astype(BF16)
    q = (_dot(hq, wxq_ref[...]) * (LOG2E / math.sqrt(X_HEAD_DIM))).astype(BF16)
    heads = []
    for hd in range(X_HEADS):
        c0 = hd * X_HEAD_DIM
        s = _nt_dot(q[:, c0:c0 + X_HEAD_DIM], kv_ref[0, :, c0:c0 + X_HEAD_DIM])
        e = jnp.exp2(s - jnp.max(s, axis=-1, keepdims=True))
        den = jnp.sum(e, axis=-1, keepdims=True)
        o = _dot(e.astype(BF16), kv_ref[0, :, D_MODEL + c0:D_MODEL + c0 + X_HEAD_DIM])
        heads.append((o / den).astype(BF16))
    xo = _dot(jnp.concatenate(heads, axis=-1), wxo_ref[...])
    o_ref[0] = x1 + _rms_rows(xo, gxo_ref[...])


def _post_call(x, od, om, os_, wout, gmp, gxp, wxq, kv, wxo, gxo, ts):
    B, S, _ = x.shape
    row = lambda n: pl.BlockSpec((1, ts, n), lambda b, i: (b, i, 0))
    full = lambda a: pl.BlockSpec(a.shape, lambda b, i: (0,) * a.ndim)
    kv_spec = pl.BlockSpec((1,) + kv.shape[1:], lambda b, i: (b, 0, 0))
    return pl.pallas_call(
        _post_kernel,
        grid=(B, S // ts),
        in_specs=[row(D_MODEL), row(256), row(256), row(512), full(wout), full(gmp), full(gxp), full(wxq),
                  kv_spec, full(wxo), full(gxo)],
        out_specs=row(D_MODEL),
        out_shape=jax.ShapeDtypeStruct(x.shape, F32),
        compiler_params=_params("parallel", "parallel"),
        name="mix_cross",
    )(x, od, om, os_, wout, gmp, gxp, wxq, kv, wxo, gxo)


def _ffn_kernel(x_ref, gpre_ref, win_ref, wout_ref, gpost_ref, o_ref):
    x = x_ref[0]
    h = _rms_rows(x, gpre_ref[...]).astype(BF16)
    gate = _dot(h, win_ref[:, 0:D_FF])
    up = _dot(h, win_ref[:, D_FF:2 * D_FF])
    f = _dot((gate * jax.nn.sigmoid(gate) * up).astype(BF16), wout_ref[...])
    o_ref[0] = x + _rms_rows(f, gpost_ref[...])


def _ffn_call(x, gpre, win, wout, gpost, ts):
    B, S, _ = x.shape
    row = pl.BlockSpec((1, ts, D_MODEL), lambda b, i: (b, i, 0))
    full = lambda a: pl.BlockSpec(a.shape, lambda b, i: (0,) * a.ndim)
    return pl.pallas_call(
        _ffn_kernel,
        grid=(B, S // ts),
        in_specs=[row, full(gpre), full(win), full(wout), full(gpost)],
        out_specs=row,
        out_shape=jax.ShapeDtypeStruct(x.shape, F32),
        compiler_params=_params("parallel", "parallel"),
        name="ffn",
    )(x, gpre, win, wout, gpost)


def _prep_weights(w_in, mla_q_norm_g, mla_w_q_up, mla_kv_norm_g, mla_w_kv_up):
    swap = np.concatenate([np.arange(16, 32), np.arange(0, 16)])
    kr = w_in[:, O_BKR:O_CQ]
    wnat = jnp.concatenate(
        [w_in[:, O_AK:O_AV], w_in[:, O_BCQ:O_BCKV], w_in[:, O_BCKV:O_BKR], kr, kr[:, swap],
         jnp.zeros((D_MODEL, 64), w_in.dtype), w_in[:, O_CK:O_CV]], axis=1)
    wtr = jnp.concatenate(
        [w_in[:, O_AQ:O_AK], w_in[:, O_AV:O_BCQ], w_in[:, O_CQ:O_CK], w_in[:, O_CV:O_END]], axis=1).T
    qup = mla_w_q_up.reshape(MLA_Q_RANK, MLA_HEADS, MLA_NOPE + MLA_ROPE)
    wqa = jnp.concatenate([qup, jnp.zeros((MLA_Q_RANK, MLA_HEADS, 32), qup.dtype)], axis=2)
    wqa = wqa.reshape(MLA_Q_RANK, MLA_HEADS * LANE).T
    wqb = qup[:, :, MLA_NOPE:][:, :, swap].reshape(MLA_Q_RANK, MLA_HEADS * MLA_ROPE).T
    kvup = mla_w_kv_up.reshape(MLA_KV_RANK, MLA_HEADS, MLA_NOPE + MLA_V)
    wk = jnp.concatenate([kvup[:, :, :MLA_NOPE], jnp.zeros((MLA_KV_RANK, MLA_HEADS, 64), kvup.dtype)], axis=2)
    wk = wk.reshape(MLA_KV_RANK, MLA_HEADS * LANE)
    wvt = kvup[:, :, MLA_NOPE:].reshape(MLA_KV_RANK, MLA_HEADS * MLA_V).T
    place = np.zeros((LANE, MLA_HEADS * LANE), np.float32)
    for hd in range(MLA_HEADS):
        place[np.arange(32), hd * LANE + MLA_NOPE + np.arange(32)] = 1.0
    return {
        "wnat": wnat.astype(BF16), "wtr": wtr.astype(BF16),
        "gq": mla_q_norm_g.reshape(1, -1), "gkv": mla_kv_norm_g.reshape(1, -1),
        "wqa": wqa.astype(BF16), "wqb": wqb.astype(BF16), "wk": wk.astype(BF16), "wvt": wvt.astype(BF16),
        "place": jnp.asarray(place, BF16),
    }


def _rope_tables(positions):
    half = MLA_ROPE // 2
    inv = ROPE_THETA ** (-jnp.arange(half, dtype=F32) / half)
    ang = positions.astype(F32)[..., None] * inv
    cos, sin = jnp.cos(ang), jnp.sin(ang)
    cos2 = jnp.concatenate([cos, cos], axis=-1)
    sin2 = jnp.concatenate([-sin, sin], axis=-1)
    tab = jnp.concatenate([cos2, sin2, jnp.zeros(cos2.shape[:2] + (64,), F32)], axis=-1)
    return {"tab": tab, "cos_t": cos2.swapaxes(1, 2), "sin_t": sin2.swapaxes(1, 2)}


def _tile(n, want):
    t = min(n, want)
    assert n % t == 0, (n, t)
    return t


def kernel(x, mem, positions, g_mix_pre, g_mix_post, w_in, diff_lambda, diff_head_g, mla_q_norm_g, mla_w_q_up,
           mla_kv_norm_g, mla_w_kv_up, swa_sinks, w_out, g_x_pre, g_x_mem, g_x_post, w_xq, w_xkv, w_xo,
           g_ffn_pre, g_ffn_post, w_ffn_in, w_ffn_out):
    B, S, _ = x.shape
    depth = w_in.shape[0]
    ts_proj, ts_post, ts_ffn = _tile(S, 512), _tile(S, 512), _tile(S, 256)
    tq, tk, tq_swa = _tile(S, 512), _tile(S, 512), _tile(S, 512)

    tabs = _rope_tables(positions)
    posq = positions.reshape(B, 1, S)
    posk = positions.reshape(B, S, 1)
    row = lambda v: v.reshape(1, -1)

    for l in range(depth):
        w = _prep_weights(w_in[l], mla_q_norm_g[l], mla_w_q_up[l], mla_kv_norm_g[l], mla_w_kv_up[l])
        names = ("qtd", "kd", "vtd", "qtm", "km", "vtm", "qts", "ks", "vts")
        p = dict(zip(names, _proj_call(x, row(g_mix_pre[l]), w, tabs, ts_proj)))
        linit = jnp.full((1, 1), 0.8 - 0.6 * math.exp(-0.3 * l), F32)
        od, om = _dense_call(p, posq, posk, diff_lambda[l], diff_head_g[l].reshape(-1, 1), linit, tq, tk)
        os_ = _swa_call(p, posq, posk, swa_sinks[l].reshape(-1, 1), tq_swa)
        kv = _memkv_call(mem, row(g_x_mem[l]), w_xkv[l].astype(BF16))
        x = _post_call(x, od, om, os_, w_out[l].astype(BF16), row(g_mix_post[l]), row(g_x_pre[l]),
                       w_xq[l].astype(BF16), kv, w_xo[l].astype(BF16), row(g_x_post[l]), ts_post)
        x = _ffn_call(x, row(g_ffn_pre[l]), w_ffn_in[l].astype(BF16), w_ffn_out[l].astype(BF16),
                      row(g_ffn_post[l]), ts_ffn)
    return x
```

```python
import functools
import math

import jax
import jax.numpy as jnp
import numpy as np
from jax import lax
from jax.experimental import pallas as pl
from jax.experimental.pallas import tpu as pltpu

F32 = jnp.float32
BF16 = jnp.bfloat16

D_MODEL = 1024
HEAD_DIM = 64
DIFF_HEADS = 4
DIFF_QK = 32
DIFF_V = 64
DIFF_MAPS = 2 * DIFF_HEADS
MLA_HEADS = 4
MLA_Q_RANK = 256
MLA_KV_RANK = 128
MLA_NOPE = 64
MLA_ROPE = 32
MLA_V = 64
SWA_Q_HEADS = 8
SWA_KV_HEADS = 2
SWA_GROUP = SWA_Q_HEADS // SWA_KV_HEADS
SWA_WINDOW = 128
X_HEADS = 4
X_HEAD_DIM = D_MODEL // X_HEADS
D_FF = 2816
N_ALIBI = DIFF_HEADS + SWA_Q_HEADS
ROPE_THETA = 10000.0
EPS = 1e-6
LOG2E = math.log2(math.e)
NEG_BIG = -1e30

LANE = 128
VMEM_LIMIT = 56 * 1024 * 1024

_OFF = np.cumsum([0, 256, 256, 256, MLA_Q_RANK, MLA_KV_RANK, MLA_ROPE, 512, 128, 128])
(O_AQ, O_AK, O_AV, O_BCQ, O_BCKV, O_BKR, O_CQ, O_CK, O_CV, O_END) = [int(v) for v in _OFF]

ONES_ROWS = 16
VT_ROWS = 64 + ONES_ROWS
AUG_LANES = 3
DIFF_AUG = 64
MLA_AUG = 96
LOGIT_LIMIT = 40.0


def _alibi_slopes():
    return [2.0 ** (-8.0 * i / N_ALIBI) for i in range(1, N_ALIBI + 1)]


SWA_SLOPES = _alibi_slopes()[:SWA_Q_HEADS]
DIFF_SLOPES = _alibi_slopes()[SWA_Q_HEADS:]


def _nt_dot(a, b):
    return lax.dot_general(a, b, (((1,), (1,)), ((), ())), preferred_element_type=F32)


def _dot(a, b):
    return jnp.dot(a, b, preferred_element_type=F32)


def _rms_rows(x, g):
    return x * lax.rsqrt(jnp.mean(x * x, axis=-1, keepdims=True) + EPS) * g


def _params(*sem, flags=None):
    return pltpu.CompilerParams(dimension_semantics=sem, vmem_limit_bytes=VMEM_LIMIT, flags=flags)


def _store_values_t(vt_ref, vt, n_heads):
    ones = jnp.ones((ONES_ROWS, vt.shape[1]), BF16)
    for hd in range(n_heads):
        vt_ref[0, hd * VT_ROWS:hd * VT_ROWS + 64, :] = vt[hd * 64:(hd + 1) * 64].astype(BF16)
        vt_ref[0, hd * VT_ROWS + 64:(hd + 1) * VT_ROWS, :] = ones


def _ones_lanes(width, first):
    lane = lax.broadcasted_iota(jnp.int32, (1, width), 1) % LANE
    return jnp.where((lane >= first) & (lane < first + AUG_LANES), 1.0, 0.0)


def _proj_kernel(x_ref, g_ref, wnat_ref, wtr_ref, gq_ref, gkv_ref, wqa_ref, wqb_ref, wk_ref, wvt_ref,
                 pl_ref, gd_ref, gm_ref, tab_ref, cost_ref, sint_ref,
                 qtd_ref, kd_ref, vtd_ref, qtm_ref, km_ref, vtm_ref, qts_ref, ks_ref, vts_ref,
                 qst_ref, kst_ref):
    h = _rms_rows(x_ref[0], g_ref[...]).astype(BF16)
    nat = _dot(h, wnat_ref[...])
    tr = _nt_dot(wtr_ref[...], h)
    ts = nat.shape[0]

    c_diff = LOG2E / math.sqrt(DIFF_QK)
    c_mla = LOG2E / math.sqrt(MLA_NOPE + MLA_ROPE)
    c_swa = LOG2E / math.sqrt(HEAD_DIM)

    qd = tr[0:256] * c_diff
    qtd_ref[0] = qd.astype(BF16)
    _store_values_t(vtd_ref, tr[256:512], DIFF_HEADS)
    qts_ref[0] = (tr[512:1024] * c_swa).astype(BF16)
    vts_ref[0] = tr[1024:1152].astype(BF16)
    kd = nat[:, 0:512]
    kd_ref[0] = (kd + _ones_lanes(512, DIFF_AUG)).astype(BF16)
    ks_ref[0] = nat[:, 1024:1152].astype(BF16)

    cq_n = _rms_rows(nat[:, 512:768], gq_ref[...]).astype(BF16)
    ckv_n = _rms_rows(nat[:, 768:896], gkv_ref[...]).astype(BF16)

    u = nat[:, 896:1024] * tab_ref[0]
    kr = (u + pltpu.roll(u, 96, 1)).astype(BF16)
    km = _dot(ckv_n, wk_ref[...]) + _dot(kr, pl_ref[...])
    km_ref[0] = (km + _ones_lanes(512, MLA_AUG)).astype(BF16)
    _store_values_t(vtm_ref, _nt_dot(wvt_ref[...], ckv_n), MLA_HEADS)

    qa = _nt_dot(wqa_ref[...], cq_n)
    qb = _nt_dot(wqb_ref[...], cq_n)
    cos_t = cost_ref[0]
    sin_t = sint_ref[0]
    for hd in range(MLA_HEADS):
        r0 = hd * LANE
        qtm_ref[0, r0:r0 + 64, :] = (qa[r0:r0 + 64] * c_mla).astype(BF16)
        rope = qa[r0 + 64:r0 + 96] * cos_t + qb[hd * 32:hd * 32 + 32] * sin_t
        qtm_ref[0, r0 + 64:r0 + 96, :] = (rope * c_mla).astype(BF16)
        qtm_ref[0, r0 + 96:r0 + 128, :] = jnp.zeros((32, rope.shape[1]), BF16)

    qn = [jnp.sum((qd * qd).reshape(DIFF_MAPS, DIFF_QK, ts), axis=1)]
    qm = (qa * qa).reshape(MLA_HEADS, LANE, ts)
    qn.append(jnp.sum(qm, axis=1) * (c_mla * c_mla))
    qn.append(jnp.zeros((16 - DIFF_MAPS - MLA_HEADS, ts), F32))
    qmax = jnp.max(jnp.concatenate(qn, axis=0), axis=1, keepdims=True)
    qst_ref[0, 0] = jnp.broadcast_to(qmax, (16, LANE))
    kn = _dot((kd * kd).astype(BF16), gd_ref[...]) + _dot((km * km).astype(BF16), gm_ref[...])
    kst_ref[0, 0] = jnp.broadcast_to(jnp.max(kn, axis=0, keepdims=True), (8, LANE))


def _proj_call(x, g, w, tabs, ts):
    B, S, _ = x.shape
    row = lambda n: pl.BlockSpec((1, ts, n), lambda b, i: (b, i, 0))
    col = lambda n: pl.BlockSpec((1, n, ts), lambda b, i: (b, 0, i))
    full = lambda a: pl.BlockSpec(a.shape, lambda b, i: (0,) * a.ndim)
    consts = [g, w["wnat"], w["wtr"], w["gq"], w["gkv"], w["wqa"], w["wqb"], w["wk"], w["wvt"], w["place"],
              w["group_d"], w["group_m"]]
    n_vt = DIFF_HEADS * VT_ROWS
    nt = S // ts
    stat = lambda r: pl.BlockSpec((1, 1, r, LANE), lambda b, i: (b, i, 0, 0))
    out_shape = [
        jax.ShapeDtypeStruct((B, 256, S), BF16), jax.ShapeDtypeStruct((B, S, 512), BF16),
        jax.ShapeDtypeStruct((B, n_vt, S), BF16), jax.ShapeDtypeStruct((B, 512, S), BF16),
        jax.ShapeDtypeStruct((B, S, 512), BF16), jax.ShapeDtypeStruct((B, n_vt, S), BF16),
        jax.ShapeDtypeStruct((B, 512, S), BF16), jax.ShapeDtypeStruct((B, S, 128), BF16),
        jax.ShapeDtypeStruct((B, 128, S), BF16),
        jax.ShapeDtypeStruct((B, nt, 16, LANE), F32), jax.ShapeDtypeStruct((B, nt, 8, LANE), F32),
    ]
    out_specs = [col(256), row(512), col(n_vt), col(512), row(512), col(n_vt), col(512), row(128), col(128),
                 stat(16), stat(8)]
    return pl.pallas_call(
        _proj_kernel,
        grid=(B, S // ts),
        in_specs=[row(D_MODEL)] + [full(a) for a in consts] + [row(128), col(32), col(32)],
        out_specs=out_specs,
        out_shape=out_shape,
        compiler_params=_params("parallel", "parallel"),
        name="in_proj",
    )(x, *consts, tabs["tab"], tabs["cos_t"], tabs["sin_t"])


STRIP = 256
SLOW_DEPTH = 3
FAST_DEPTH = 2
N_MAPS = DIFF_MAPS + MLA_HEADS


def _split3(x):
    hi = x.astype(BF16).astype(F32)
    mid = (x - hi).astype(BF16).astype(F32)
    lo = (x - hi - mid).astype(BF16).astype(F32)
    return hi, mid, lo


def _dense_kernel(fast_ref, qtd_ref, kd_ref, vtd_ref, qtm_ref, km_ref, vtm_ref, posq_ref, posk_ref,
                  lam_ref, hg_ref, linit_ref, od_ref, om_ref,
                  wq_ref, m_ref, alpha_ref, acc_ref, s_ref):
    b, qi, ki = pl.program_id(0), pl.program_id(1), pl.program_id(2)
    tq = qtd_ref.shape[2]
    n_slots = s_ref.shape[0]

    @pl.when(ki == 0)
    def _init():
        m_ref[...] = jnp.full(m_ref.shape, NEG_BIG, F32)
        acc_ref[...] = jnp.zeros(acc_ref.shape, F32)
        for mp in range(DIFF_MAPS):
            q = qtd_ref[0, mp * DIFF_QK:(mp + 1) * DIFF_QK, :]
            above = jnp.zeros(((mp % 2) * DIFF_QK, tq), BF16)
            below = jnp.zeros((LANE - (mp % 2 + 1) * DIFF_QK, tq), BF16)
            wq_ref[mp] = jnp.concatenate([above, q, below] if mp % 2 else [q, below], axis=0)
        for hd in range(MLA_HEADS):
            wq_ref[DIFF_MAPS + hd] = qtm_ref[0, hd * LANE:(hd + 1) * LANE, :]

    def key_slab(mp):
        if mp < DIFF_MAPS:
            return kd_ref[0, :, (mp // 2) * LANE:(mp // 2 + 1) * LANE]
        hd = mp - DIFF_MAPS
        return km_ref[0, :, hd * LANE:(hd + 1) * LANE]

    def values_t(mp):
        if mp < DIFF_MAPS:
            return vtd_ref[0, (mp // 2) * VT_ROWS:(mp // 2 + 1) * VT_ROWS, :]
        hd = mp - DIFF_MAPS
        return vtm_ref[0, hd * VT_ROWS:(hd + 1) * VT_ROWS, :]

    def set_shift_rows(shift):
        rows = lax.broadcasted_iota(jnp.int32, (ONES_ROWS, tq), 0)
        for mp in range(N_MAPS):
            hi, mid, lo = _split3(shift[mp:mp + 1, :])
            block = jnp.where(rows == 0, hi, jnp.where(rows == 1, mid, jnp.where(rows == 2, lo, 0.0)))
            first = DIFF_AUG if mp < DIFF_MAPS else MLA_AUG
            wq_ref[mp, first:first + ONES_ROWS, :] = block.astype(BF16)

    def work_items():
        items = []
        for st in range(tq // STRIP):
            lanes = slice(st * STRIP, (st + 1) * STRIP)
            dist = jnp.abs(posk_ref[0] - posq_ref[0, :, lanes]).astype(F32)
            bias = [dist * (slope * LOG2E) for slope in DIFF_SLOPES]
            items += [(mp, lanes, bias) for mp in range(N_MAPS)]
        return items

    def logits(mp, lanes, bias):
        s = _dot(key_slab(mp), wq_ref[mp, :, lanes])
        return s - bias[mp // 2] if mp < DIFF_MAPS else s

    fast = fast_ref[b, qi, ki] != 0

    @pl.when(fast)
    def _fast_step():
        m_old = m_ref[...]
        m_eff = jnp.maximum(m_old, -LOGIT_LIMIT)
        alpha_ref[...] = jnp.exp2(m_old - m_eff)
        m_ref[...] = m_eff
        set_shift_rows(-m_eff)
        items = work_items()
        vals = {}
        for t in range(len(items) + FAST_DEPTH):
            if t < len(items):
                vals[t] = logits(*items[t])
            if t >= FAST_DEPTH:
                mp, lanes, _ = items[t - FAST_DEPTH]
                p = jnp.exp2(vals.pop(t - FAST_DEPTH)).astype(BF16)
                acc_ref[mp, :, lanes] = alpha_ref[mp:mp + 1, lanes] * acc_ref[mp, :, lanes] + _dot(values_t(mp), p)

    @pl.when(jnp.logical_not(fast))
    def _slow_step():
        set_shift_rows(jnp.zeros(m_ref.shape, F32))
        items = work_items()

        def stage_qk(j):
            s_ref[j % n_slots] = logits(*items[j])

        def stage_max(j):
            mp, lanes, _ = items[j]
            m_old = m_ref[mp:mp + 1, lanes]
            m_new = jnp.maximum(m_old, jnp.max(s_ref[j % n_slots], axis=0, keepdims=True))
            m_ref[mp:mp + 1, lanes] = m_new
            return m_new, jnp.exp2(m_old - m_new)

        def stage_pv(j, m_new, alpha):
            mp, lanes, _ = items[j]
            p = jnp.exp2((s_ref[j % n_slots] - m_new).astype(BF16))
            acc_ref[mp, :, lanes] = alpha * acc_ref[mp, :, lanes] + _dot(values_t(mp), p)

        stats = {}
        for t in range(len(items) + SLOW_DEPTH):
            if t < len(items):
                stage_qk(t)
            if 1 <= t <= len(items):
                stats[t - 1] = stage_max(t - 1)
            if t >= SLOW_DEPTH:
                stage_pv(t - SLOW_DEPTH, *stats.pop(t - SLOW_DEPTH))

    @pl.when(ki == pl.num_programs(2) - 1)
    def _finish():
        lp = lam_ref[...]
        lam_init = linit_ref[...]
        lam = (jnp.exp(jnp.sum(lp[0:1] * lp[1:2], axis=-1, keepdims=True))
               - jnp.exp(jnp.sum(lp[2:3] * lp[3:4], axis=-1, keepdims=True)) + lam_init)
        def normalised(idx):
            return acc_ref[idx, 0:64, :] / acc_ref[idx, 64:65, :]

        outs = []
        for hd in range(DIFF_HEADS):
            o = normalised(2 * hd) - lam * normalised(2 * hd + 1)
            o = o * lax.rsqrt(jnp.mean(o * o, axis=0, keepdims=True) + EPS) * hg_ref[...]
            outs.append(o * (1.0 - lam_init))
        od_ref[0] = jnp.concatenate(outs, axis=0).T.astype(BF16)
        outs = [normalised(DIFF_MAPS + hd) for hd in range(MLA_HEADS)]
        om_ref[0] = jnp.concatenate(outs, axis=0).T.astype(BF16)


def _dense_call(p, fast, posq, posk, lam, hg, linit, tq, tk):
    B, _, S = p["qtd"].shape
    qcol = lambda n: pl.BlockSpec((1, n, tq), lambda b, i, k, f: (b, 0, i))
    krow = lambda n: pl.BlockSpec((1, tk, n), lambda b, i, k, f: (b, k, 0))
    kcol = lambda n: pl.BlockSpec((1, n, tk), lambda b, i, k, f: (b, 0, k))
    full = lambda a: pl.BlockSpec(a.shape, lambda b, i, k, f: (0,) * a.ndim)
    n_vt = DIFF_HEADS * VT_ROWS
    grid_spec = pltpu.PrefetchScalarGridSpec(
        num_scalar_prefetch=1,
        grid=(B, S // tq, S // tk),
        in_specs=[qcol(256), krow(512), kcol(n_vt), qcol(512), krow(512), kcol(n_vt),
                  qcol(1), krow(1), full(lam), full(hg), full(linit)],
        out_specs=[pl.BlockSpec((1, tq, 256), lambda b, i, k, f: (b, i, 0))] * 2,
        scratch_shapes=[pltpu.VMEM((N_MAPS, LANE, tq), BF16),
                        pltpu.VMEM((16, tq), F32), pltpu.VMEM((16, tq), F32),
                        pltpu.VMEM((N_MAPS, VT_ROWS, tq), F32),
                        pltpu.VMEM((SLOW_DEPTH + 2, tk, STRIP), F32)],
    )
    return pl.pallas_call(
        _dense_kernel,
        grid_spec=grid_spec,
        out_shape=[jax.ShapeDtypeStruct((B, S, 256), BF16)] * 2,
        compiler_params=_params("parallel", "parallel", "arbitrary"),
        name="dense_attn",
    )(fast, p["qtd"], p["kd"], p["vtd"], p["qtm"], p["km"], p["vtm"], posq, posk, lam, hg, linit)


def _swa_kernel(qt_ref, kc_ref, kp_ref, kn_ref, vc_ref, vp_ref, vn_ref,
                posq_ref, pkc_ref, pkp_ref, pkn_ref, sink_ref, o_ref):
    i = pl.program_id(1)
    last = pl.num_programs(1) - 1
    tq = qt_ref.shape[2]
    w = SWA_WINDOW
    posq = posq_ref[0]
    qidx = lax.broadcasted_iota(jnp.int32, (1, tq), 1)
    segs = [(kp_ref, pkp_ref, vp_ref, -w, i > 0), (kc_ref, pkc_ref, vc_ref, 0, None),
            (kn_ref, pkn_ref, vn_ref, tq, i < last)]
    geo = []
    for k_ref, pk_ref, _, start, valid in segs:
        n = k_ref.shape[1]
        kidx = lax.broadcasted_iota(jnp.int32, (n, 1), 0) + start
        if valid is not None:
            kidx = kidx + jnp.where(valid, 0, 4 * (tq + w))
        ok = jnp.abs(kidx - qidx) <= w
        geo.append((jnp.abs(pk_ref[0] - posq).astype(F32), ok))
    zeros = jnp.zeros((HEAD_DIM, tq), BF16)
    outs = []
    for hq in range(SWA_Q_HEADS):
        kv = hq // SWA_GROUP
        q_h = qt_ref[0, hq * HEAD_DIM:(hq + 1) * HEAD_DIM, :]
        wq = jnp.concatenate([q_h, zeros] if kv == 0 else [zeros, q_h], axis=0)
        sink = sink_ref[hq:hq + 1, :] * LOG2E
        ss = []
        m = jnp.broadcast_to(sink, (1, tq))
        for (k_ref, _, _, _, _), (dist, ok) in zip(segs, geo):
            s = _dot(k_ref[0], wq) - dist * (SWA_SLOPES[hq] * LOG2E)
            s = jnp.where(ok, s, NEG_BIG)
            m = jnp.maximum(m, jnp.max(s, axis=0, keepdims=True))
            ss.append(s)
        den = jnp.exp2(sink - m)
        acc = jnp.zeros((HEAD_DIM, tq), F32)
        for s, (_, _, v_ref, _, _) in zip(ss, segs):
            e = jnp.exp2(s - m)
            den = den + jnp.sum(e, axis=0, keepdims=True)
            acc = acc + _dot(v_ref[0, kv * HEAD_DIM:(kv + 1) * HEAD_DIM, :], e.astype(BF16))
        outs.append(acc / den)
    o_ref[0] = jnp.concatenate(outs, axis=0).T.astype(BF16)


def _swa_call(p, posq, posk, sinks, tq):
    B, _, S = p["qts"].shape
    w = SWA_WINDOW
    r = tq // w
    nblk = S // w
    prev = lambda b, i: (b, jnp.maximum(i * r - 1, 0), 0)
    nxt = lambda b, i: (b, jnp.minimum((i + 1) * r, nblk - 1), 0)
    prev_t = lambda b, i: (b, 0, jnp.maximum(i * r - 1, 0))
    nxt_t = lambda b, i: (b, 0, jnp.minimum((i + 1) * r, nblk - 1))
    cur = lambda b, i: (b, i, 0)
    cur_t = lambda b, i: (b, 0, i)
    return pl.pallas_call(
        _swa_kernel,
        grid=(B, S // tq),
        in_specs=[pl.BlockSpec((1, 512, tq), cur_t),
                  pl.BlockSpec((1, tq, 128), cur), pl.BlockSpec((1, w, 128), prev), pl.BlockSpec((1, w, 128), nxt),
                  pl.BlockSpec((1, 128, tq), cur_t), pl.BlockSpec((1, 128, w), prev_t),
                  pl.BlockSpec((1, 128, w), nxt_t),
                  pl.BlockSpec((1, 1, tq), cur_t),
                  pl.BlockSpec((1, tq, 1), cur), pl.BlockSpec((1, w, 1), prev), pl.BlockSpec((1, w, 1), nxt),
                  pl.BlockSpec(sinks.shape, lambda b, i: (0, 0))],
        out_specs=pl.BlockSpec((1, tq, 512), cur),
        out_shape=jax.ShapeDtypeStruct((B, S, 512), BF16),
        compiler_params=_params("parallel", "parallel"),
        name="swa_attn",
    )(p["qts"], p["ks"], p["ks"], p["ks"], p["vts"], p["vts"], p["vts"],
      posq, posk, posk, posk, sinks)


def _memkv_kernel(mem_ref, g_ref, w_ref, kv_ref):
    mem_n = _rms_rows(mem_ref[0], g_ref[...]).astype(BF16)
    kv_ref[0] = _dot(mem_n, w_ref[...]).astype(BF16)


def _memkv_call(mem, g, w):
    B, M, _ = mem.shape
    return pl.pallas_call(
        _memkv_kernel,
        grid=(B,),
        in_specs=[pl.BlockSpec((1, M, D_MODEL), lambda b: (b, 0, 0)),
                  pl.BlockSpec(g.shape, lambda b: (0, 0)), pl.BlockSpec(w.shape, lambda b: (0, 0))],
        out_specs=pl.BlockSpec((1, M, 2 * D_MODEL), lambda b: (b, 0, 0)),
        out_shape=jax.ShapeDtypeStruct((B, M, 2 * D_MODEL), BF16),
        compiler_params=_params("parallel"),
        name="mem_kv",
    )(mem, g, w)


def _post_kernel(x_ref, od_ref, om_ref, os_ref, wout_ref, gmp_ref, gxp_ref, wxq_ref, kv_ref, wxo_ref,
                 gxo_ref, o_ref):
    mix = (_dot(od_ref[0], wout_ref[0:256, :]) + _dot(om_ref[0], wout_ref[256:512, :])
           + _dot(os_ref[0], wout_ref[512:1024, :]))
    x1 = x_ref[0] + _rms_rows(mix, gmp_ref[...])
    hq = _rms_rows(x1, gxp_ref[...]).astype(BF16)
    q = (_dot(hq, wxq_ref[...]) * (LOG2E / math.sqrt(X_HEAD_DIM))).astype(BF16)
    heads = []
    for hd in range(X_HEADS):
        c0 = hd * X_HEAD_DIM
        s = _nt_dot(q[:, c0:c0 + X_HEAD_DIM], kv_ref[0, :, c0:c0 + X_HEAD_DIM])
        e = jnp.exp2(s - jnp.max(s, axis=-1, keepdims=True))
        den = jnp.sum(e, axis=-1, keepdims=True)
        o = _dot(e.astype(BF16), kv_ref[0, :, D_MODEL + c0:D_MODEL + c0 + X_HEAD_DIM])
        heads.append((o / den).astype(BF16))
    xo = _dot(jnp.concatenate(heads, axis=-1), wxo_ref[...])
    o_ref[0] = x1 + _rms_rows(xo, gxo_ref[...])


def _post_call(x, od, om, os_, wout, gmp, gxp, wxq, kv, wxo, gxo, ts):
    B, S, _ = x.shape
    row = lambda n: pl.BlockSpec((1, ts, n), lambda b, i: (b, i, 0))
    full = lambda a: pl.BlockSpec(a.shape, lambda b, i: (0,) * a.ndim)
    kv_spec = pl.BlockSpec((1,) + kv.shape[1:], lambda b, i: (b, 0, 0))
    return pl.pallas_call(
        _post_kernel,
        grid=(B, S // ts),
        in_specs=[row(D_MODEL), row(256), row(256), row(512), full(wout), full(gmp), full(gxp), full(wxq),
                  kv_spec, full(wxo), full(gxo)],
        out_specs=row(D_MODEL),
        out_shape=jax.ShapeDtypeStruct(x.shape, F32),
        compiler_params=_params("parallel", "parallel"),
        name="mix_cross",
    )(x, od, om, os_, wout, gmp, gxp, wxq, kv, wxo, gxo)


def _ffn_kernel(x_ref, gpre_ref, win_ref, wout_ref, gpost_ref, o_ref):
    x = x_ref[0]
    h = _rms_rows(x, gpre_ref[...]).astype(BF16)
    gate = _dot(h, win_ref[:, 0:D_FF])
    up = _dot(h, win_ref[:, D_FF:2 * D_FF])
    f = _dot((gate * jax.nn.sigmoid(gate) * up).astype(BF16), wout_ref[...])
    o_ref[0] = x + _rms_rows(f, gpost_ref[...])


def _ffn_call(x, gpre, win, wout, gpost, ts):
    B, S, _ = x.shape
    row = pl.BlockSpec((1, ts, D_MODEL), lambda b, i: (b, i, 0))
    full = lambda a: pl.BlockSpec(a.shape, lambda b, i: (0,) * a.ndim)
    return pl.pallas_call(
        _ffn_kernel,
        grid=(B, S // ts),
        in_specs=[row, full(gpre), full(win), full(wout), full(gpost)],
        out_specs=row,
        out_shape=jax.ShapeDtypeStruct(x.shape, F32),
        compiler_params=_params("parallel", "parallel"),
        name="ffn",
    )(x, gpre, win, wout, gpost)


def _prep_weights(w_in, mla_q_norm_g, mla_w_q_up, mla_kv_norm_g, mla_w_kv_up):
    swap = np.concatenate([np.arange(16, 32), np.arange(0, 16)])
    kr = w_in[:, O_BKR:O_CQ]
    zeros64 = jnp.zeros((D_MODEL, 64), w_in.dtype)
    ak = w_in[:, O_AK:O_AV]
    ak_slabs = [a for hd in range(DIFF_HEADS) for a in (ak[:, 64 * hd:64 * (hd + 1)], zeros64)]
    wnat = jnp.concatenate(
        ak_slabs + [w_in[:, O_BCQ:O_BCKV], w_in[:, O_BCKV:O_BKR], kr, kr[:, swap], zeros64,
                    w_in[:, O_CK:O_CV]], axis=1)
    wtr = jnp.concatenate(
        [w_in[:, O_AQ:O_AK], w_in[:, O_AV:O_BCQ], w_in[:, O_CQ:O_CK], w_in[:, O_CV:O_END]], axis=1).T
    qup = mla_w_q_up.reshape(MLA_Q_RANK, MLA_HEADS, MLA_NOPE + MLA_ROPE)
    wqa = jnp.concatenate([qup, jnp.zeros((MLA_Q_RANK, MLA_HEADS, 32), qup.dtype)], axis=2)
    wqa = wqa.reshape(MLA_Q_RANK, MLA_HEADS * LANE).T
    wqb = qup[:, :, MLA_NOPE:][:, :, swap].reshape(MLA_Q_RANK, MLA_HEADS * MLA_ROPE).T
    kvup = mla_w_kv_up.reshape(MLA_KV_RANK, MLA_HEADS, MLA_NOPE + MLA_V)
    wk = jnp.concatenate([kvup[:, :, :MLA_NOPE], jnp.zeros((MLA_KV_RANK, MLA_HEADS, 64), kvup.dtype)], axis=2)
    wk = wk.reshape(MLA_KV_RANK, MLA_HEADS * LANE)
    wvt = kvup[:, :, MLA_NOPE:].reshape(MLA_KV_RANK, MLA_HEADS * MLA_V).T
    place = np.zeros((LANE, MLA_HEADS * LANE), np.float32)
    group_d = np.zeros((DIFF_HEADS * LANE, LANE), np.float32)
    group_m = np.zeros((MLA_HEADS * LANE, LANE), np.float32)
    for hd in range(MLA_HEADS):
        place[np.arange(32), hd * LANE + MLA_NOPE + np.arange(32)] = 1.0
        group_m[hd * LANE + np.arange(MLA_NOPE + MLA_ROPE), DIFF_MAPS + hd] = 1.0
    for mp in range(DIFF_MAPS):
        group_d[(mp // 2) * LANE + (mp % 2) * DIFF_QK + np.arange(DIFF_QK), mp] = 1.0
    return {
        "group_d": jnp.asarray(group_d, BF16), "group_m": jnp.asarray(group_m, BF16),
        "wnat": wnat.astype(BF16), "wtr": wtr.astype(BF16),
        "gq": mla_q_norm_g.reshape(1, -1), "gkv": mla_kv_norm_g.reshape(1, -1),
        "wqa": wqa.astype(BF16), "wqb": wqb.astype(BF16), "wk": wk.astype(BF16), "wvt": wvt.astype(BF16),
        "place": jnp.asarray(place, BF16),
    }


def _rope_tables(positions):
    half = MLA_ROPE // 2
    inv = ROPE_THETA ** (-jnp.arange(half, dtype=F32) / half)
    ang = positions.astype(F32)[..., None] * inv
    cos, sin = jnp.cos(ang), jnp.sin(ang)
    cos2 = jnp.concatenate([cos, cos], axis=-1)
    sin2 = jnp.concatenate([-sin, sin], axis=-1)
    tab = jnp.concatenate([cos2, sin2, jnp.zeros(cos2.shape[:2] + (64,), F32)], axis=-1)
    return {"tab": tab, "cos_t": cos2.swapaxes(1, 2), "sin_t": sin2.swapaxes(1, 2)}


def _fast_tiles(qstat, kstat, nq, nk):
    B = qstat.shape[0]
    qn = qstat[:, :, :N_MAPS, 0].reshape(B, nq, -1, N_MAPS).max(axis=2)
    kn = kstat[:, :, 0, :N_MAPS].reshape(B, nk, -1, N_MAPS).max(axis=2)
    bound2 = qn[:, :, None, :] * kn[:, None, :, :]
    margin = 0.97
    return jnp.all(bound2 <= (margin * LOGIT_LIMIT) ** 2, axis=-1).astype(jnp.int32)


def _tile(n, want):
    t = min(n, want)
    assert n % t == 0, (n, t)
    return t


def kernel(x, mem, positions, g_mix_pre, g_mix_post, w_in, diff_lambda, diff_head_g, mla_q_norm_g, mla_w_q_up,
           mla_kv_norm_g, mla_w_kv_up, swa_sinks, w_out, g_x_pre, g_x_mem, g_x_post, w_xq, w_xkv, w_xo,
           g_ffn_pre, g_ffn_post, w_ffn_in, w_ffn_out):
    B, S, _ = x.shape
    depth = w_in.shape[0]
    ts_proj, ts_post, ts_ffn = _tile(S, 512), _tile(S, 512), _tile(S, 256)
    tq, tk, tq_swa = _tile(S, 512), _tile(S, 512), _tile(S, 512)

    tabs = _rope_tables(positions)
    posq = positions.reshape(B, 1, S)
    posk = positions.reshape(B, S, 1)
    row = lambda v: v.reshape(1, -1)

    for l in range(depth):
        w = _prep_weights(w_in[l], mla_q_norm_g[l], mla_w_q_up[l], mla_kv_norm_g[l], mla_w_kv_up[l])
        names = ("qtd", "kd", "vtd", "qtm", "km", "vtm", "qts", "ks", "vts", "qstat", "kstat")
        p = dict(zip(names, _proj_call(x, row(g_mix_pre[l]), w, tabs, ts_proj)))
        linit = jnp.full((1, 1), 0.8 - 0.6 * math.exp(-0.3 * l), F32)
        fast = _fast_tiles(p["qstat"], p["kstat"], S // tq, S // tk)
        od, om = _dense_call(p, fast, posq, posk, diff_lambda[l], diff_head_g[l].reshape(-1, 1), linit, tq, tk)
        os_ = _swa_call(p, posq, posk, swa_sinks[l].reshape(-1, 1), tq_swa)
        kv = _memkv_call(mem, row(g_x_mem[l]), w_xkv[l].astype(BF16))
        x = _post_call(x, od, om, os_, w_out[l].astype(BF16), row(g_mix_post[l]), row(g_x_pre[l]),
                       w_xq[l].astype(BF16), kv, w_xo[l].astype(BF16), row(g_x_post[l]), ts_post)
        x = _ffn_call(x, row(g_ffn_pre[l]), w_ffn_in[l].astype(BF16), w_ffn_out[l].astype(BF16),
                      row(g_ffn_post[l]), ts_ffn)
    return x
```

```python
import functools
import math

import jax
import jax.numpy as jnp
import numpy as np
from jax import lax
from jax.experimental import pallas as pl
from jax.experimental.pallas import tpu as pltpu

F32 = jnp.float32
BF16 = jnp.bfloat16

D_MODEL = 1024
HEAD_DIM = 64
DIFF_HEADS = 4
DIFF_QK = 32
DIFF_V = 64
DIFF_MAPS = 2 * DIFF_HEADS
MLA_HEADS = 4
MLA_Q_RANK = 256
MLA_KV_RANK = 128
MLA_NOPE = 64
MLA_ROPE = 32
MLA_V = 64
SWA_Q_HEADS = 8
SWA_KV_HEADS = 2
SWA_GROUP = SWA_Q_HEADS // SWA_KV_HEADS
SWA_WINDOW = 128
X_HEADS = 4
X_HEAD_DIM = D_MODEL // X_HEADS
D_FF = 2816
N_ALIBI = DIFF_HEADS + SWA_Q_HEADS
ROPE_THETA = 10000.0
EPS = 1e-6
LOG2E = math.log2(math.e)
NEG_BIG = -1e30

LANE = 128
VMEM_LIMIT = 56 * 1024 * 1024

_OFF = np.cumsum([0, 256, 256, 256, MLA_Q_RANK, MLA_KV_RANK, MLA_ROPE, 512, 128, 128])
(O_AQ, O_AK, O_AV, O_BCQ, O_BCKV, O_BKR, O_CQ, O_CK, O_CV, O_END) = [int(v) for v in _OFF]

ONES_ROWS = 16
VT_ROWS = 64 + ONES_ROWS
AUG_LANES = 3
DIFF_AUG = 64
MLA_AUG = 96
LOGIT_LIMIT = 40.0


def _alibi_slopes():
    return [2.0 ** (-8.0 * i / N_ALIBI) for i in range(1, N_ALIBI + 1)]


SWA_SLOPES = _alibi_slopes()[:SWA_Q_HEADS]
DIFF_SLOPES = _alibi_slopes()[SWA_Q_HEADS:]


def _nt_dot(a, b):
    return lax.dot_general(a, b, (((1,), (1,)), ((), ())), preferred_element_type=F32)


def _dot(a, b):
    return jnp.dot(a, b, preferred_element_type=F32)


def _rms_rows(x, g):
    return x * lax.rsqrt(jnp.mean(x * x, axis=-1, keepdims=True) + EPS) * g


def _params(*sem, flags=None):
    return pltpu.CompilerParams(dimension_semantics=sem, vmem_limit_bytes=VMEM_LIMIT, flags=flags)


def _store_values_t(vt_ref, vt, n_heads):
    ones = jnp.ones((ONES_ROWS, vt.shape[1]), BF16)
    for hd in range(n_heads):
        vt_ref[0, hd * VT_ROWS:hd * VT_ROWS + 64, :] = vt[hd * 64:(hd + 1) * 64].astype(BF16)
        vt_ref[0, hd * VT_ROWS + 64:(hd + 1) * VT_ROWS, :] = ones


def _ones_lanes(width, first):
    lane = lax.broadcasted_iota(jnp.int32, (1, width), 1) % LANE
    return jnp.where((lane >= first) & (lane < first + AUG_LANES), 1.0, 0.0)


def _proj_kernel(x_ref, g_ref, wnat_ref, wtr_ref, gq_ref, gkv_ref, wqa_ref, wqb_ref, wk_ref, wvt_ref,
                 pl_ref, gd_ref, gm_ref, tab_ref, cost_ref, sint_ref,
                 qtd_ref, kd_ref, vtd_ref, qtm_ref, km_ref, vtm_ref, qts_ref, ks_ref, vts_ref,
                 qst_ref, kst_ref):
    h = _rms_rows(x_ref[0], g_ref[...]).astype(BF16)
    nat = _dot(h, wnat_ref[...])
    tr = _nt_dot(wtr_ref[...], h)
    ts = nat.shape[0]

    c_diff = LOG2E / math.sqrt(DIFF_QK)
    c_mla = LOG2E / math.sqrt(MLA_NOPE + MLA_ROPE)
    c_swa = LOG2E / math.sqrt(HEAD_DIM)

    qd = tr[0:256] * c_diff
    qtd_ref[0] = qd.astype(BF16)
    _store_values_t(vtd_ref, tr[256:512], DIFF_HEADS)
    qts_ref[0] = (tr[512:1024] * c_swa).astype(BF16)
    vts_ref[0] = tr[1024:1152].astype(BF16)
    kd = nat[:, 0:512]
    kd_ref[0] = (kd + _ones_lanes(512, DIFF_AUG)).astype(BF16)
    ks_ref[0] = nat[:, 1024:1152].astype(BF16)

    cq_n = _rms_rows(nat[:, 512:768], gq_ref[...]).astype(BF16)
    ckv_n = _rms_rows(nat[:, 768:896], gkv_ref[...]).astype(BF16)

    u = nat[:, 896:1024] * tab_ref[0]
    kr = (u + pltpu.roll(u, 96, 1)).astype(BF16)
    km = _dot(ckv_n, wk_ref[...]) + _dot(kr, pl_ref[...])
    km_ref[0] = (km + _ones_lanes(512, MLA_AUG)).astype(BF16)
    _store_values_t(vtm_ref, _nt_dot(wvt_ref[...], ckv_n), MLA_HEADS)

    qa = _nt_dot(wqa_ref[...], cq_n)
    qb = _nt_dot(wqb_ref[...], cq_n)
    cos_t = cost_ref[0]
    sin_t = sint_ref[0]
    for hd in range(MLA_HEADS):
        r0 = hd * LANE
        qtm_ref[0, r0:r0 + 64, :] = (qa[r0:r0 + 64] * c_mla).astype(BF16)
        rope = qa[r0 + 64:r0 + 96] * cos_t + qb[hd * 32:hd * 32 + 32] * sin_t
        qtm_ref[0, r0 + 64:r0 + 96, :] = (rope * c_mla).astype(BF16)
        qtm_ref[0, r0 + 96:r0 + 128, :] = jnp.zeros((32, rope.shape[1]), BF16)

    qn = [jnp.sum((qd * qd).reshape(DIFF_MAPS, DIFF_QK, ts), axis=1)]
    qm = (qa * qa).reshape(MLA_HEADS, LANE, ts)
    qn.append(jnp.sum(qm, axis=1) * (c_mla * c_mla))
    qn.append(jnp.zeros((16 - DIFF_MAPS - MLA_HEADS, ts), F32))
    qmax = jnp.max(jnp.concatenate(qn, axis=0), axis=1, keepdims=True)
    qst_ref[0, 0] = jnp.broadcast_to(qmax, (16, LANE))
    kn = _dot((kd * kd).astype(BF16), gd_ref[...]) + _dot((km * km).astype(BF16), gm_ref[...])
    kst_ref[0, 0] = jnp.broadcast_to(jnp.max(kn, axis=0, keepdims=True), (8, LANE))


def _proj_call(x, g, w, tabs, ts):
    B, S, _ = x.shape
    row = lambda n: pl.BlockSpec((1, ts, n), lambda b, i: (b, i, 0))
    col = lambda n: pl.BlockSpec((1, n, ts), lambda b, i: (b, 0, i))
    full = lambda a: pl.BlockSpec(a.shape, lambda b, i: (0,) * a.ndim)
    consts = [g, w["wnat"], w["wtr"], w["gq"], w["gkv"], w["wqa"], w["wqb"], w["wk"], w["wvt"], w["place"],
              w["group_d"], w["group_m"]]
    n_vt = DIFF_HEADS * VT_ROWS
    nt = S // ts
    stat = lambda r: pl.BlockSpec((1, 1, r, LANE), lambda b, i: (b, i, 0, 0))
    out_shape = [
        jax.ShapeDtypeStruct((B, 256, S), BF16), jax.ShapeDtypeStruct((B, S, 512), BF16),
        jax.ShapeDtypeStruct((B, n_vt, S), BF16), jax.ShapeDtypeStruct((B, 512, S), BF16),
        jax.ShapeDtypeStruct((B, S, 512), BF16), jax.ShapeDtypeStruct((B, n_vt, S), BF16),
        jax.ShapeDtypeStruct((B, 512, S), BF16), jax.ShapeDtypeStruct((B, S, 128), BF16),
        jax.ShapeDtypeStruct((B, 128, S), BF16),
        jax.ShapeDtypeStruct((B, nt, 16, LANE), F32), jax.ShapeDtypeStruct((B, nt, 8, LANE), F32),
    ]
    out_specs = [col(256), row(512), col(n_vt), col(512), row(512), col(n_vt), col(512), row(128), col(128),
                 stat(16), stat(8)]
    return pl.pallas_call(
        _proj_kernel,
        grid=(B, S // ts),
        in_specs=[row(D_MODEL)] + [full(a) for a in consts] + [row(128), col(32), col(32)],
        out_specs=out_specs,
        out_shape=out_shape,
        compiler_params=_params("parallel", "parallel"),
        name="in_proj",
    )(x, *consts, tabs["tab"], tabs["cos_t"], tabs["sin_t"])


STRIP = 256
SLOW_DEPTH = 3
FAST_DEPTH = 2
N_MAPS = DIFF_MAPS + MLA_HEADS


def _split3(x):
    hi = x.astype(BF16).astype(F32)
    mid = (x - hi).astype(BF16).astype(F32)
    lo = (x - hi - mid).astype(BF16).astype(F32)
    return hi, mid, lo


def _dense_kernel(fast_ref, qtd_ref, kd_ref, vtd_ref, qtm_ref, km_ref, vtm_ref, posq_ref, posk_ref,
                  lam_ref, hg_ref, linit_ref, od_ref, om_ref,
                  wq_ref, m_ref, alpha_ref, acc_ref, s_ref):
    b, qi, ki = pl.program_id(0), pl.program_id(1), pl.program_id(2)
    tq = qtd_ref.shape[2]
    n_slots = s_ref.shape[0]

    @pl.when(ki == 0)
    def _init():
        m_ref[...] = jnp.full(m_ref.shape, NEG_BIG, F32)
        acc_ref[...] = jnp.zeros(acc_ref.shape, F32)
        for mp in range(DIFF_MAPS):
            q = qtd_ref[0, mp * DIFF_QK:(mp + 1) * DIFF_QK, :]
            above = jnp.zeros(((mp % 2) * DIFF_QK, tq), BF16)
            below = jnp.zeros((LANE - (mp % 2 + 1) * DIFF_QK, tq), BF16)
            wq_ref[mp] = jnp.concatenate([above, q, below] if mp % 2 else [q, below], axis=0)
        for hd in range(MLA_HEADS):
            wq_ref[DIFF_MAPS + hd] = qtm_ref[0, hd * LANE:(hd + 1) * LANE, :]

    def key_slab(mp):
        if mp < DIFF_MAPS:
            return kd_ref[0, :, (mp // 2) * LANE:(mp // 2 + 1) * LANE]
        hd = mp - DIFF_MAPS
        return km_ref[0, :, hd * LANE:(hd + 1) * LANE]

    def values_t(mp):
        if mp < DIFF_MAPS:
            return vtd_ref[0, (mp // 2) * VT_ROWS:(mp // 2 + 1) * VT_ROWS, :]
        hd = mp - DIFF_MAPS
        return vtm_ref[0, hd * VT_ROWS:(hd + 1) * VT_ROWS, :]

    def set_shift_rows(shift):
        rows = lax.broadcasted_iota(jnp.int32, (ONES_ROWS, tq), 0)
        for mp in range(N_MAPS):
            hi, mid, lo = _split3(shift[mp:mp + 1, :])
            block = jnp.where(rows == 0, hi, jnp.where(rows == 1, mid, jnp.where(rows == 2, lo, 0.0)))
            first = DIFF_AUG if mp < DIFF_MAPS else MLA_AUG
            wq_ref[mp, first:first + ONES_ROWS, :] = block.astype(BF16)

    def work_items():
        items = []
        for st in range(tq // STRIP):
            lanes = slice(st * STRIP, (st + 1) * STRIP)
            dist = jnp.abs(posk_ref[0] - posq_ref[0, :, lanes]).astype(F32)
            bias = [dist * (slope * LOG2E) for slope in DIFF_SLOPES]
            items += [(mp, lanes, bias) for mp in range(N_MAPS)]
        return items

    def logits(mp, lanes, bias):
        s = _dot(key_slab(mp), wq_ref[mp, :, lanes])
        return s - bias[mp // 2] if mp < DIFF_MAPS else s

    fast = fast_ref[b, qi, ki] != 0

    @pl.when(fast)
    def _fast_step():
        m_old = m_ref[...]
        m_eff = jnp.maximum(m_old, -LOGIT_LIMIT)
        alpha_ref[...] = jnp.exp2(m_old - m_eff)
        m_ref[...] = m_eff
        set_shift_rows(-m_eff)
        items = work_items()
        vals = {}
        for t in range(len(items) + FAST_DEPTH):
            if t < len(items):
                vals[t] = logits(*items[t])
            if t >= FAST_DEPTH:
                mp, lanes, _ = items[t - FAST_DEPTH]
                p = jnp.exp2(vals.pop(t - FAST_DEPTH)).astype(BF16)
                acc_ref[mp, :, lanes] = alpha_ref[mp:mp + 1, lanes] * acc_ref[mp, :, lanes] + _dot(values_t(mp), p)

    @pl.when(jnp.logical_not(fast))
    def _slow_step():
        set_shift_rows(jnp.zeros(m_ref.shape, F32))
        items = work_items()

        def stage_qk(j):
            s_ref[j % n_slots] = logits(*items[j])

        def stage_max(j):
            mp, lanes, _ = items[j]
            m_old = m_ref[mp:mp + 1, lanes]
            m_new = jnp.maximum(m_old, jnp.max(s_ref[j % n_slots], axis=0, keepdims=True))
            m_ref[mp:mp + 1, lanes] = m_new
            return m_new, jnp.exp2(m_old - m_new)

        def stage_pv(j, m_new, alpha):
            mp, lanes, _ = items[j]
            p = jnp.exp2((s_ref[j % n_slots] - m_new).astype(BF16))
            acc_ref[mp, :, lanes] = alpha * acc_ref[mp, :, lanes] + _dot(values_t(mp), p)

        stats = {}
        for t in range(len(items) + SLOW_DEPTH):
            if t < len(items):
                stage_qk(t)
            if 1 <= t <= len(items):
                stats[t - 1] = stage_max(t - 1)
            if t >= SLOW_DEPTH:
                stage_pv(t - SLOW_DEPTH, *stats.pop(t - SLOW_DEPTH))

    @pl.when(ki == pl.num_programs(2) - 1)
    def _finish():
        lp = lam_ref[...]
        lam_init = linit_ref[...]
        lam = (jnp.exp(jnp.sum(lp[0:1] * lp[1:2], axis=-1, keepdims=True))
               - jnp.exp(jnp.sum(lp[2:3] * lp[3:4], axis=-1, keepdims=True)) + lam_init)
        def normalised(idx):
            return acc_ref[idx, 0:64, :] / acc_ref[idx, 64:65, :]

        outs = []
        for hd in range(DIFF_HEADS):
            o = normalised(2 * hd) - lam * normalised(2 * hd + 1)
            o = o * lax.rsqrt(jnp.mean(o * o, axis=0, keepdims=True) + EPS) * hg_ref[...]
            outs.append(o * (1.0 - lam_init))
        od_ref[0] = jnp.concatenate(outs, axis=0).T.astype(BF16)
        outs = [normalised(DIFF_MAPS + hd) for hd in range(MLA_HEADS)]
        om_ref[0] = jnp.concatenate(outs, axis=0).T.astype(BF16)


def _dense_call(p, fast, posq, posk, lam, hg, linit, tq, tk):
    B, _, S = p["qtd"].shape
    qcol = lambda n: pl.BlockSpec((1, n, tq), lambda b, i, k, f: (b, 0, i))
    krow = lambda n: pl.BlockSpec((1, tk, n), lambda b, i, k, f: (b, k, 0))
    kcol = lambda n: pl.BlockSpec((1, n, tk), lambda b, i, k, f: (b, 0, k))
    full = lambda a: pl.BlockSpec(a.shape, lambda b, i, k, f: (0,) * a.ndim)
    n_vt = DIFF_HEADS * VT_ROWS
    grid_spec = pltpu.PrefetchScalarGridSpec(
        num_scalar_prefetch=1,
        grid=(B, S // tq, S // tk),
        in_specs=[qcol(256), krow(512), kcol(n_vt), qcol(512), krow(512), kcol(n_vt),
                  qcol(1), krow(1), full(lam), full(hg), full(linit)],
        out_specs=[pl.BlockSpec((1, tq, 256), lambda b, i, k, f: (b, i, 0))] * 2,
        scratch_shapes=[pltpu.VMEM((N_MAPS, LANE, tq), BF16),
                        pltpu.VMEM((16, tq), F32), pltpu.VMEM((16, tq), F32),
                        pltpu.VMEM((N_MAPS, VT_ROWS, tq), F32),
                        pltpu.VMEM((SLOW_DEPTH + 2, tk, STRIP), F32)],
    )
    return pl.pallas_call(
        _dense_kernel,
        grid_spec=grid_spec,
        out_shape=[jax.ShapeDtypeStruct((B, S, 256), BF16)] * 2,
        compiler_params=_params("parallel", "parallel", "arbitrary"),
        name="dense_attn",
    )(fast, p["qtd"], p["kd"], p["vtd"], p["qtm"], p["km"], p["vtm"], posq, posk, lam, hg, linit)


SWA_DEPTH = 5


def _swa_kernel(qt_ref, kc_ref, kp_ref, kn_ref, vc_ref, vp_ref, vn_ref,
                posq_ref, pkc_ref, pkp_ref, pkn_ref, sink_ref, o_ref):
    i = pl.program_id(1)
    last = pl.num_programs(1) - 1
    tq = qt_ref.shape[2]
    w = SWA_WINDOW
    n_ext = tq + 2 * w
    k_ext = jnp.concatenate([kp_ref[0], kc_ref[0], kn_ref[0]], axis=0)
    vt_ext = jnp.concatenate([vp_ref[0], vc_ref[0], vn_ref[0]], axis=1)
    pos_ext = jnp.concatenate([pkp_ref[0], pkc_ref[0], pkn_ref[0]], axis=0)
    row = lax.broadcasted_iota(jnp.int32, (n_ext, 1), 0)
    far = 4 * n_ext
    kidx = (row - w + jnp.where(row < w, jnp.where(i == 0, far, 0), 0)
            + jnp.where(row >= tq + w, jnp.where(i == last, far, 0), 0))
    zeros = jnp.zeros((HEAD_DIM, tq), BF16)
    wqs = []
    for hq in range(SWA_Q_HEADS):
        q_h = qt_ref[0, hq * HEAD_DIM:(hq + 1) * HEAD_DIM, :]
        wqs.append(jnp.concatenate([q_h, zeros] if hq // SWA_GROUP == 0 else [zeros, q_h], axis=0))

    def item_logits(r, hq, dist, ok):
        sink = sink_ref[hq:hq + 1, :] * LOG2E
        s = _dot(k_ext[r * w:(r + 3) * w], wqs[hq][:, r * w:(r + 1) * w]) - dist * (SWA_SLOPES[hq] * LOG2E)
        s = jnp.where(ok, s, NEG_BIG)
        m = jnp.maximum(sink, jnp.max(s, axis=0, keepdims=True))
        return s, m, sink

    def item_output(r, hq, s, m, sink):
        kv = hq // SWA_GROUP
        e = jnp.exp2(s - m)
        den = jnp.exp2(sink - m) + jnp.sum(e, axis=0, keepdims=True)
        return _dot(vt_ext[kv * HEAD_DIM:(kv + 1) * HEAD_DIM, r * w:(r + 3) * w], e.astype(BF16)) / den

    items = []
    for r in range(tq // w):
        dist = jnp.abs(pos_ext[r * w:(r + 3) * w] - posq_ref[0, :, r * w:(r + 1) * w]).astype(F32)
        qidx = lax.broadcasted_iota(jnp.int32, (1, w), 1) + r * w
        ok = jnp.abs(kidx[r * w:(r + 3) * w] - qidx) <= w
        items += [(r, hq, dist, ok) for hq in range(SWA_Q_HEADS)]
    outs = {}
    ahead = {}
    for t in range(len(items) + SWA_DEPTH):
        if t < len(items):
            ahead[t] = item_logits(*items[t])
        if t >= SWA_DEPTH:
            r, hq, _, _ = items[t - SWA_DEPTH]
            outs[(hq, r)] = item_output(r, hq, *ahead.pop(t - SWA_DEPTH))
    heads = [jnp.concatenate([outs[(hq, r)] for r in range(tq // w)], axis=1) for hq in range(SWA_Q_HEADS)]
    o_ref[0] = jnp.concatenate(heads, axis=0).T.astype(BF16)


def _swa_call(p, posq, posk, sinks, tq):
    B, _, S = p["qts"].shape
    w = SWA_WINDOW
    r = tq // w
    nblk = S // w
    prev = lambda b, i: (b, jnp.maximum(i * r - 1, 0), 0)
    nxt = lambda b, i: (b, jnp.minimum((i + 1) * r, nblk - 1), 0)
    prev_t = lambda b, i: (b, 0, jnp.maximum(i * r - 1, 0))
    nxt_t = lambda b, i: (b, 0, jnp.minimum((i + 1) * r, nblk - 1))
    cur = lambda b, i: (b, i, 0)
    cur_t = lambda b, i: (b, 0, i)
    return pl.pallas_call(
        _swa_kernel,
        grid=(B, S // tq),
        in_specs=[pl.BlockSpec((1, 512, tq), cur_t),
                  pl.BlockSpec((1, tq, 128), cur), pl.BlockSpec((1, w, 128), prev), pl.BlockSpec((1, w, 128), nxt),
                  pl.BlockSpec((1, 128, tq), cur_t), pl.BlockSpec((1, 128, w), prev_t),
                  pl.BlockSpec((1, 128, w), nxt_t),
                  pl.BlockSpec((1, 1, tq), cur_t),
                  pl.BlockSpec((1, tq, 1), cur), pl.BlockSpec((1, w, 1), prev), pl.BlockSpec((1, w, 1), nxt),
                  pl.BlockSpec(sinks.shape, lambda b, i: (0, 0))],
        out_specs=pl.BlockSpec((1, tq, 512), cur),
        out_shape=jax.ShapeDtypeStruct((B, S, 512), BF16),
        compiler_params=_params("parallel", "parallel"),
        name="swa_attn",
    )(p["qts"], p["ks"], p["ks"], p["ks"], p["vts"], p["vts"], p["vts"],
      posq, posk, posk, posk, sinks)


def _memkv_kernel(mem_ref, g_ref, w_ref, kv_ref):
    mem_n = _rms_rows(mem_ref[0], g_ref[...]).astype(BF16)
    kv_ref[0] = _dot(mem_n, w_ref[...]).astype(BF16)


def _memkv_call(mem, g, w):
    B, M, _ = mem.shape
    return pl.pallas_call(
        _memkv_kernel,
        grid=(B,),
        in_specs=[pl.BlockSpec((1, M, D_MODEL), lambda b: (b, 0, 0)),
                  pl.BlockSpec(g.shape, lambda b: (0, 0)), pl.BlockSpec(w.shape, lambda b: (0, 0))],
        out_specs=pl.BlockSpec((1, M, 2 * D_MODEL), lambda b: (b, 0, 0)),
        out_shape=jax.ShapeDtypeStruct((B, M, 2 * D_MODEL), BF16),
        compiler_params=_params("parallel"),
        name="mem_kv",
    )(mem, g, w)


def _post_kernel(x_ref, od_ref, om_ref, os_ref, wout_ref, gmp_ref, gxp_ref, wxq_ref, kv_ref, wxo_ref,
                 gxo_ref, o_ref):
    mix = (_dot(od_ref[0], wout_ref[0:256, :]) + _dot(om_ref[0], wout_ref[256:512, :])
           + _dot(os_ref[0], wout_ref[512:1024, :]))
    x1 = x_ref[0] + _rms_rows(mix, gmp_ref[...])
    hq = _rms_rows(x1, gxp_ref[...]).astype(BF16)
    q = (_dot(hq, wxq_ref[...]) * (LOG2E / math.sqrt(X_HEAD_DIM))).astype(BF16)
    heads = []
    for hd in range(X_HEADS):
        c0 = hd * X_HEAD_DIM
        s = _nt_dot(q[:, c0:c0 + X_HEAD_DIM], kv_ref[0, :, c0:c0 + X_HEAD_DIM])
        e = jnp.exp2(s - jnp.max(s, axis=-1, keepdims=True))
        den = jnp.sum(e, axis=-1, keepdims=True)
        o = _dot(e.astype(BF16), kv_ref[0, :, D_MODEL + c0:D_MODEL + c0 + X_HEAD_DIM])
        heads.append((o / den).astype(BF16))
    xo = _dot(jnp.concatenate(heads, axis=-1), wxo_ref[...])
    o_ref[0] = x1 + _rms_rows(xo, gxo_ref[...])


def _post_call(x, od, om, os_, wout, gmp, gxp, wxq, kv, wxo, gxo, ts):
    B, S, _ = x.shape
    row = lambda n: pl.BlockSpec((1, ts, n), lambda b, i: (b, i, 0))
    full = lambda a: pl.BlockSpec(a.shape, lambda b, i: (0,) * a.ndim)
    kv_spec = pl.BlockSpec((1,) + kv.shape[1:], lambda b, i: (b, 0, 0))
    return pl.pallas_call(
        _post_kernel,
        grid=(B, S // ts),
        in_specs=[row(D_MODEL), row(256), row(256), row(512), full(wout), full(gmp), full(gxp), full(wxq),
                  kv_spec, full(wxo), full(gxo)],
        out_specs=row(D_MODEL),
        out_shape=jax.ShapeDtypeStruct(x.shape, F32),
        compiler_params=_params("parallel", "parallel"),
        name="mix_cross",
    )(x, od, om, os_, wout, gmp, gxp, wxq, kv, wxo, gxo)


def _ffn_kernel(x_ref, gpre_ref, win_ref, wout_ref, gpost_ref, o_ref):
    x = x_ref[0]
    h = _rms_rows(x, gpre_ref[...]).astype(BF16)
    gate = _dot(h, win_ref[:, 0:D_FF])
    up = _dot(h, win_ref[:, D_FF:2 * D_FF])
    f = _dot((gate * jax.nn.sigmoid(gate) * up).astype(BF16), wout_ref[...])
    o_ref[0] = x + _rms_rows(f, gpost_ref[...])


def _ffn_call(x, gpre, win, wout, gpost, ts):
    B, S, _ = x.shape
    row = pl.BlockSpec((1, ts, D_MODEL), lambda b, i: (b, i, 0))
    full = lambda a: pl.BlockSpec(a.shape, lambda b, i: (0,) * a.ndim)
    return pl.pallas_call(
        _ffn_kernel,
        grid=(B, S // ts),
        in_specs=[row, full(gpre), full(win), full(wout), full(gpost)],
        out_specs=row,
        out_shape=jax.ShapeDtypeStruct(x.shape, F32),
        compiler_params=_params("parallel", "parallel"),
        name="ffn",
    )(x, gpre, win, wout, gpost)


def _prep_weights(w_in, mla_q_norm_g, mla_w_q_up, mla_kv_norm_g, mla_w_kv_up):
    swap = np.concatenate([np.arange(16, 32), np.arange(0, 16)])
    kr = w_in[:, O_BKR:O_CQ]
    zeros64 = jnp.zeros((D_MODEL, 64), w_in.dtype)
    ak = w_in[:, O_AK:O_AV]
    ak_slabs = [a for hd in range(DIFF_HEADS) for a in (ak[:, 64 * hd:64 * (hd + 1)], zeros64)]
    wnat = jnp.concatenate(
        ak_slabs + [w_in[:, O_BCQ:O_BCKV], w_in[:, O_BCKV:O_BKR], kr, kr[:, swap], zeros64,
                    w_in[:, O_CK:O_CV]], axis=1)
    wtr = jnp.concatenate(
        [w_in[:, O_AQ:O_AK], w_in[:, O_AV:O_BCQ], w_in[:, O_CQ:O_CK], w_in[:, O_CV:O_END]], axis=1).T
    qup = mla_w_q_up.reshape(MLA_Q_RANK, MLA_HEADS, MLA_NOPE + MLA_ROPE)
    wqa = jnp.concatenate([qup, jnp.zeros((MLA_Q_RANK, MLA_HEADS, 32), qup.dtype)], axis=2)
    wqa = wqa.reshape(MLA_Q_RANK, MLA_HEADS * LANE).T
    wqb = qup[:, :, MLA_NOPE:][:, :, swap].reshape(MLA_Q_RANK, MLA_HEADS * MLA_ROPE).T
    kvup = mla_w_kv_up.reshape(MLA_KV_RANK, MLA_HEADS, MLA_NOPE + MLA_V)
    wk = jnp.concatenate([kvup[:, :, :MLA_NOPE], jnp.zeros((MLA_KV_RANK, MLA_HEADS, 64), kvup.dtype)], axis=2)
    wk = wk.reshape(MLA_KV_RANK, MLA_HEADS * LANE)
    wvt = kvup[:, :, MLA_NOPE:].reshape(MLA_KV_RANK, MLA_HEADS * MLA_V).T
    place = np.zeros((LANE, MLA_HEADS * LANE), np.float32)
    group_d = np.zeros((DIFF_HEADS * LANE, LANE), np.float32)
    group_m = np.zeros((MLA_HEADS * LANE, LANE), np.float32)
    for hd in range(MLA_HEADS):
        place[np.arange(32), hd * LANE + MLA_NOPE + np.arange(32)] = 1.0
        group_m[hd * LANE + np.arange(MLA_NOPE + MLA_ROPE), DIFF_MAPS + hd] = 1.0
    for mp in range(DIFF_MAPS):
        group_d[(mp // 2) * LANE + (mp % 2) * DIFF_QK + np.arange(DIFF_QK), mp] = 1.0
    return {
        "group_d": jnp.asarray(group_d, BF16), "group_m": jnp.asarray(group_m, BF16),
        "wnat": wnat.astype(BF16), "wtr": wtr.astype(BF16),
        "gq": mla_q_norm_g.reshape(1, -1), "gkv": mla_kv_norm_g.reshape(1, -1),
        "wqa": wqa.astype(BF16), "wqb": wqb.astype(BF16), "wk": wk.astype(BF16), "wvt": wvt.astype(BF16),
        "place": jnp.asarray(place, BF16),
    }


def _rope_tables(positions):
    half = MLA_ROPE // 2
    inv = ROPE_THETA ** (-jnp.arange(half, dtype=F32) / half)
    ang = positions.astype(F32)[..., None] * inv
    cos, sin = jnp.cos(ang), jnp.sin(ang)
    cos2 = jnp.concatenate([cos, cos], axis=-1)
    sin2 = jnp.concatenate([-sin, sin], axis=-1)
    tab = jnp.concatenate([cos2, sin2, jnp.zeros(cos2.shape[:2] + (64,), F32)], axis=-1)
    return {"tab": tab, "cos_t": cos2.swapaxes(1, 2), "sin_t": sin2.swapaxes(1, 2)}


def _fast_tiles(qstat, kstat, nq, nk):
    B = qstat.shape[0]
    qn = qstat[:, :, :N_MAPS, 0].reshape(B, nq, -1, N_MAPS).max(axis=2)
    kn = kstat[:, :, 0, :N_MAPS].reshape(B, nk, -1, N_MAPS).max(axis=2)
    bound2 = qn[:, :, None, :] * kn[:, None, :, :]
    margin = 0.97
    return jnp.all(bound2 <= (margin * LOGIT_LIMIT) ** 2, axis=-1).astype(jnp.int32)


def _tile(n, want):
    t = min(n, want)
    assert n % t == 0, (n, t)
    return t


def kernel(x, mem, positions, g_mix_pre, g_mix_post, w_in, diff_lambda, diff_head_g, mla_q_norm_g, mla_w_q_up,
           mla_kv_norm_g, mla_w_kv_up, swa_sinks, w_out, g_x_pre, g_x_mem, g_x_post, w_xq, w_xkv, w_xo,
           g_ffn_pre, g_ffn_post, w_ffn_in, w_ffn_out):
    B, S, _ = x.shape
    depth = w_in.shape[0]
    ts_proj, ts_post, ts_ffn = _tile(S, 512), _tile(S, 512), _tile(S, 256)
    tq, tk, tq_swa = _tile(S, 1024), _tile(S, 512), _tile(S, 512)

    tabs = _rope_tables(positions)
    posq = positions.reshape(B, 1, S)
    posk = positions.reshape(B, S, 1)
    row = lambda v: v.reshape(1, -1)

    for l in range(depth):
        w = _prep_weights(w_in[l], mla_q_norm_g[l], mla_w_q_up[l], mla_kv_norm_g[l], mla_w_kv_up[l])
        names = ("qtd", "kd", "vtd", "qtm", "km", "vtm", "qts", "ks", "vts", "qstat", "kstat")
        p = dict(zip(names, _proj_call(x, row(g_mix_pre[l]), w, tabs, ts_proj)))
        linit = jnp.full((1, 1), 0.8 - 0.6 * math.exp(-0.3 * l), F32)
        fast = _fast_tiles(p["qstat"], p["kstat"], S // tq, S // tk)
        od, om = _dense_call(p, fast, posq, posk, diff_lambda[l], diff_head_g[l].reshape(-1, 1), linit, tq, tk)
        os_ = _swa_call(p, posq, posk, swa_sinks[l].reshape(-1, 1), tq_swa)
        kv = _memkv_call(mem, row(g_x_mem[l]), w_xkv[l].astype(BF16))
        x = _post_call(x, od, om, os_, w_out[l].astype(BF16), row(g_mix_post[l]), row(g_x_pre[l]),
                       w_xq[l].astype(BF16), kv, w_xo[l].astype(BF16), row(g_x_post[l]), ts_post)
        x = _ffn_call(x, row(g_ffn_pre[l]), w_ffn_in[l].astype(BF16), w_ffn_out[l].astype(BF16),
                      row(g_ffn_post[l]), ts_ffn)
    return x
```

```python
import functools
import math

import jax
import jax.numpy as jnp
import numpy as np
from jax import lax
from jax.experimental import pallas as pl
from jax.experimental.pallas import tpu as pltpu

F32 = jnp.float32
BF16 = jnp.bfloat16

D_MODEL = 1024
HEAD_DIM = 64
DIFF_HEADS = 4
DIFF_QK = 32
DIFF_V = 64
DIFF_MAPS = 2 * DIFF_HEADS
MLA_HEADS = 4
MLA_Q_RANK = 256
MLA_KV_RANK = 128
MLA_NOPE = 64
MLA_ROPE = 32
MLA_V = 64
SWA_Q_HEADS = 8
SWA_KV_HEADS = 2
SWA_GROUP = SWA_Q_HEADS // SWA_KV_HEADS
SWA_WINDOW = 128
X_HEADS = 4
X_HEAD_DIM = D_MODEL // X_HEADS
D_FF = 2816
N_ALIBI = DIFF_HEADS + SWA_Q_HEADS
ROPE_THETA = 10000.0
EPS = 1e-6
LOG2E = math.log2(math.e)
NEG_BIG = -1e30

LANE = 128
VMEM_LIMIT = 56 * 1024 * 1024

_OFF = np.cumsum([0, 256, 256, 256, MLA_Q_RANK, MLA_KV_RANK, MLA_ROPE, 512, 128, 128])
(O_AQ, O_AK, O_AV, O_BCQ, O_BCKV, O_BKR, O_CQ, O_CK, O_CV, O_END) = [int(v) for v in _OFF]

ONES_ROWS = 16
VT_ROWS = 64 + ONES_ROWS
AUG_LANES = 3
DIFF_AUG = 64
MLA_AUG = 96
LOGIT_LIMIT = 40.0


def _alibi_slopes():
    return [2.0 ** (-8.0 * i / N_ALIBI) for i in range(1, N_ALIBI + 1)]


SWA_SLOPES = _alibi_slopes()[:SWA_Q_HEADS]
DIFF_SLOPES = _alibi_slopes()[SWA_Q_HEADS:]


def _nt_dot(a, b):
    return lax.dot_general(a, b, (((1,), (1,)), ((), ())), preferred_element_type=F32)


def _dot(a, b):
    return jnp.dot(a, b, preferred_element_type=F32)


def _rms_rows(x, g):
    return x * lax.rsqrt(jnp.mean(x * x, axis=-1, keepdims=True) + EPS) * g


def _params(*sem, flags=None):
    return pltpu.CompilerParams(dimension_semantics=sem, vmem_limit_bytes=VMEM_LIMIT, flags=flags)


def _store_values_t(vt_ref, vt, n_heads):
    ones = jnp.ones((ONES_ROWS, vt.shape[1]), BF16)
    for hd in range(n_heads):
        vt_ref[0, hd * VT_ROWS:hd * VT_ROWS + 64, :] = vt[hd * 64:(hd + 1) * 64].astype(BF16)
        vt_ref[0, hd * VT_ROWS + 64:(hd + 1) * VT_ROWS, :] = ones


def _ones_lanes(width, first):
    lane = lax.broadcasted_iota(jnp.int32, (1, width), 1) % LANE
    return jnp.where((lane >= first) & (lane < first + AUG_LANES), 1.0, 0.0)


def _proj_kernel(x_ref, g_ref, wnat_ref, wtr_ref, gq_ref, gkv_ref, wqa_ref, wqb_ref, wk_ref, wvt_ref,
                 pl_ref, gd_ref, gm_ref, tab_ref, cost_ref, sint_ref,
                 qtd_ref, kd_ref, vtd_ref, qtm_ref, km_ref, vtm_ref, qts_ref, ks_ref, vts_ref,
                 qst_ref, kst_ref):
    h = _rms_rows(x_ref[0], g_ref[...]).astype(BF16)
    nat = _dot(h, wnat_ref[...])
    tr = _nt_dot(wtr_ref[...], h)
    ts = nat.shape[0]

    c_diff = LOG2E / math.sqrt(DIFF_QK)
    c_mla = LOG2E / math.sqrt(MLA_NOPE + MLA_ROPE)
    c_swa = LOG2E / math.sqrt(HEAD_DIM)

    qd = tr[0:256] * c_diff
    qtd_ref[0] = qd.astype(BF16)
    _store_values_t(vtd_ref, tr[256:512], DIFF_HEADS)
    qts_ref[0] = (tr[512:1024] * c_swa).astype(BF16)
    vts_ref[0] = tr[1024:1152].astype(BF16)
    kd = nat[:, 0:512]
    kd_ref[0] = (kd + _ones_lanes(512, DIFF_AUG)).astype(BF16)
    ks_ref[0] = nat[:, 1024:1152].astype(BF16)

    cq_n = _rms_rows(nat[:, 512:768], gq_ref[...]).astype(BF16)
    ckv_n = _rms_rows(nat[:, 768:896], gkv_ref[...]).astype(BF16)

    u = nat[:, 896:1024] * tab_ref[0]
    kr = (u + pltpu.roll(u, 96, 1)).astype(BF16)
    km = _dot(ckv_n, wk_ref[...]) + _dot(kr, pl_ref[...])
    km_ref[0] = (km + _ones_lanes(512, MLA_AUG)).astype(BF16)
    _store_values_t(vtm_ref, _nt_dot(wvt_ref[...], ckv_n), MLA_HEADS)

    qa = _nt_dot(wqa_ref[...], cq_n)
    qb = _nt_dot(wqb_ref[...], cq_n)
    cos_t = cost_ref[0]
    sin_t = sint_ref[0]
    for hd in range(MLA_HEADS):
        r0 = hd * LANE
        qtm_ref[0, r0:r0 + 64, :] = (qa[r0:r0 + 64] * c_mla).astype(BF16)
        rope = qa[r0 + 64:r0 + 96] * cos_t + qb[hd * 32:hd * 32 + 32] * sin_t
        qtm_ref[0, r0 + 64:r0 + 96, :] = (rope * c_mla).astype(BF16)
        qtm_ref[0, r0 + 96:r0 + 128, :] = jnp.zeros((32, rope.shape[1]), BF16)

    qn = [jnp.sum((qd * qd).reshape(DIFF_MAPS, DIFF_QK, ts), axis=1)]
    qm = (qa * qa).reshape(MLA_HEADS, LANE, ts)
    qn.append(jnp.sum(qm, axis=1) * (c_mla * c_mla))
    qn.append(jnp.zeros((16 - DIFF_MAPS - MLA_HEADS, ts), F32))
    qmax = jnp.max(jnp.concatenate(qn, axis=0), axis=1, keepdims=True)
    qst_ref[0, 0] = jnp.broadcast_to(qmax, (16, LANE))
    kn = _dot((kd * kd).astype(BF16), gd_ref[...]) + _dot((km * km).astype(BF16), gm_ref[...])
    kst_ref[0, 0] = jnp.broadcast_to(jnp.max(kn, axis=0, keepdims=True), (8, LANE))


def _proj_call(x, g, w, tabs, ts):
    B, S, _ = x.shape
    row = lambda n: pl.BlockSpec((1, ts, n), lambda b, i: (b, i, 0))
    col = lambda n: pl.BlockSpec((1, n, ts), lambda b, i: (b, 0, i))
    full = lambda a: pl.BlockSpec(a.shape, lambda b, i: (0,) * a.ndim)
    consts = [g, w["wnat"], w["wtr"], w["gq"], w["gkv"], w["wqa"], w["wqb"], w["wk"], w["wvt"], w["place"],
              w["group_d"], w["group_m"]]
    n_vt = DIFF_HEADS * VT_ROWS
    nt = S // ts
    stat = lambda r: pl.BlockSpec((1, 1, r, LANE), lambda b, i: (b, i, 0, 0))
    out_shape = [
        jax.ShapeDtypeStruct((B, 256, S), BF16), jax.ShapeDtypeStruct((B, S, 512), BF16),
        jax.ShapeDtypeStruct((B, n_vt, S), BF16), jax.ShapeDtypeStruct((B, 512, S), BF16),
        jax.ShapeDtypeStruct((B, S, 512), BF16), jax.ShapeDtypeStruct((B, n_vt, S), BF16),
        jax.ShapeDtypeStruct((B, 512, S), BF16), jax.ShapeDtypeStruct((B, S, 128), BF16),
        jax.ShapeDtypeStruct((B, 128, S), BF16),
        jax.ShapeDtypeStruct((B, nt, 16, LANE), F32), jax.ShapeDtypeStruct((B, nt, 8, LANE), F32),
    ]
    out_specs = [col(256), row(512), col(n_vt), col(512), row(512), col(n_vt), col(512), row(128), col(128),
                 stat(16), stat(8)]
    return pl.pallas_call(
        _proj_kernel,
        grid=(B, S // ts),
        in_specs=[row(D_MODEL)] + [full(a) for a in consts] + [row(128), col(32), col(32)],
        out_specs=out_specs,
        out_shape=out_shape,
        compiler_params=_params("parallel", "parallel"),
        name="in_proj",
    )(x, *consts, tabs["tab"], tabs["cos_t"], tabs["sin_t"])


STRIP = 256
SLOW_DEPTH = 3
FAST_DEPTH = 2
FAST_KEYS = 512
N_MAPS = DIFF_MAPS + MLA_HEADS


def _split3(x):
    hi = x.astype(BF16).astype(F32)
    mid = (x - hi).astype(BF16).astype(F32)
    lo = (x - hi - mid).astype(BF16).astype(F32)
    return hi, mid, lo


def _dense_kernel(fast_ref, qtd_ref, kd_ref, vtd_ref, qtm_ref, km_ref, vtm_ref, posq_ref, posk_ref,
                  lam_ref, hg_ref, linit_ref, od_ref, om_ref,
                  wq_ref, m_ref, alpha_ref, acc_ref, s_ref):
    b, qi, ki = pl.program_id(0), pl.program_id(1), pl.program_id(2)
    tq = qtd_ref.shape[2]
    n_slots = s_ref.shape[0]

    @pl.when(ki == 0)
    def _init():
        m_ref[...] = jnp.full(m_ref.shape, NEG_BIG, F32)
        acc_ref[...] = jnp.zeros(acc_ref.shape, F32)
        for mp in range(DIFF_MAPS):
            q = qtd_ref[0, mp * DIFF_QK:(mp + 1) * DIFF_QK, :]
            above = jnp.zeros(((mp % 2) * DIFF_QK, tq), BF16)
            below = jnp.zeros((LANE - (mp % 2 + 1) * DIFF_QK, tq), BF16)
            wq_ref[mp] = jnp.concatenate([above, q, below] if mp % 2 else [q, below], axis=0)
        for hd in range(MLA_HEADS):
            wq_ref[DIFF_MAPS + hd] = qtm_ref[0, hd * LANE:(hd + 1) * LANE, :]

    def key_slab(mp, keys):
        if mp < DIFF_MAPS:
            return kd_ref[0, keys, (mp // 2) * LANE:(mp // 2 + 1) * LANE]
        hd = mp - DIFF_MAPS
        return km_ref[0, keys, hd * LANE:(hd + 1) * LANE]

    def values_t(mp, keys):
        if mp < DIFF_MAPS:
            return vtd_ref[0, (mp // 2) * VT_ROWS:(mp // 2 + 1) * VT_ROWS, keys]
        hd = mp - DIFF_MAPS
        return vtm_ref[0, hd * VT_ROWS:(hd + 1) * VT_ROWS, keys]

    def set_shift_rows(shift):
        rows = lax.broadcasted_iota(jnp.int32, (ONES_ROWS, tq), 0)
        for mp in range(N_MAPS):
            hi, mid, lo = _split3(shift[mp:mp + 1, :])
            block = jnp.where(rows == 0, hi, jnp.where(rows == 1, mid, jnp.where(rows == 2, lo, 0.0)))
            first = DIFF_AUG if mp < DIFF_MAPS else MLA_AUG
            wq_ref[mp, first:first + ONES_ROWS, :] = block.astype(BF16)

    def work_items(chunk):
        items = []
        for st in range(tq // STRIP):
            lanes = slice(st * STRIP, (st + 1) * STRIP)
            for kc in range(posk_ref.shape[1] // chunk):
                keys = slice(kc * chunk, (kc + 1) * chunk)
                dist = jnp.abs(posk_ref[0, keys] - posq_ref[0, :, lanes]).astype(F32)
                bias = [dist * (slope * LOG2E) for slope in DIFF_SLOPES]
                items += [(mp, lanes, keys, bias) for mp in range(N_MAPS)]
        return items

    def logits(mp, lanes, keys, bias):
        s = _dot(key_slab(mp, keys), wq_ref[mp, :, lanes])
        return s - bias[mp // 2] if mp < DIFF_MAPS else s

    fast = fast_ref[b, qi, ki] != 0

    @pl.when(fast)
    def _fast_step():
        m_old = m_ref[...]
        m_eff = jnp.maximum(m_old, -LOGIT_LIMIT)
        alpha_ref[...] = jnp.exp2(m_old - m_eff)
        m_ref[...] = m_eff
        set_shift_rows(-m_eff)
        items = work_items(FAST_KEYS)
        vals = {}
        for t in range(len(items) + FAST_DEPTH):
            if t < len(items):
                vals[t] = logits(*items[t])
            if t >= FAST_DEPTH:
                mp, lanes, keys, _ = items[t - FAST_DEPTH]
                p = jnp.exp2(vals.pop(t - FAST_DEPTH)).astype(BF16)
                old = acc_ref[mp, :, lanes]
                if keys.start == 0:
                    old = alpha_ref[mp:mp + 1, lanes] * old
                acc_ref[mp, :, lanes] = old + _dot(values_t(mp, keys), p)

    @pl.when(jnp.logical_not(fast))
    def _slow_step():
        set_shift_rows(jnp.zeros(m_ref.shape, F32))
        items = work_items(posk_ref.shape[1])

        def stage_qk(j):
            s_ref[j % n_slots] = logits(*items[j])

        def stage_max(j):
            mp, lanes, _, _ = items[j]
            m_old = m_ref[mp:mp + 1, lanes]
            m_new = jnp.maximum(m_old, jnp.max(s_ref[j % n_slots], axis=0, keepdims=True))
            m_ref[mp:mp + 1, lanes] = m_new
            return m_new, jnp.exp2(m_old - m_new)

        def stage_pv(j, m_new, alpha):
            mp, lanes, keys, _ = items[j]
            p = jnp.exp2((s_ref[j % n_slots] - m_new).astype(BF16))
            acc_ref[mp, :, lanes] = alpha * acc_ref[mp, :, lanes] + _dot(values_t(mp, keys), p)

        stats = {}
        for t in range(len(items) + SLOW_DEPTH):
            if t < len(items):
                stage_qk(t)
            if 1 <= t <= len(items):
                stats[t - 1] = stage_max(t - 1)
            if t >= SLOW_DEPTH:
                stage_pv(t - SLOW_DEPTH, *stats.pop(t - SLOW_DEPTH))

    @pl.when(ki == pl.num_programs(2) - 1)
    def _finish():
        lp = lam_ref[...]
        lam_init = linit_ref[...]
        lam = (jnp.exp(jnp.sum(lp[0:1] * lp[1:2], axis=-1, keepdims=True))
               - jnp.exp(jnp.sum(lp[2:3] * lp[3:4], axis=-1, keepdims=True)) + lam_init)
        def normalised(idx):
            return acc_ref[idx, 0:64, :] / acc_ref[idx, 64:65, :]

        outs = []
        for hd in range(DIFF_HEADS):
            o = normalised(2 * hd) - lam * normalised(2 * hd + 1)
            o = o * lax.rsqrt(jnp.mean(o * o, axis=0, keepdims=True) + EPS) * hg_ref[...]
            outs.append(o * (1.0 - lam_init))
        od_ref[0] = jnp.concatenate(outs, axis=0).T.astype(BF16)
        outs = [normalised(DIFF_MAPS + hd) for hd in range(MLA_HEADS)]
        om_ref[0] = jnp.concatenate(outs, axis=0).T.astype(BF16)


def _dense_call(p, fast, posq, posk, lam, hg, linit, tq, tk):
    B, _, S = p["qtd"].shape
    qcol = lambda n: pl.BlockSpec((1, n, tq), lambda b, i, k, f: (b, 0, i))
    krow = lambda n: pl.BlockSpec((1, tk, n), lambda b, i, k, f: (b, k, 0))
    kcol = lambda n: pl.BlockSpec((1, n, tk), lambda b, i, k, f: (b, 0, k))
    full = lambda a: pl.BlockSpec(a.shape, lambda b, i, k, f: (0,) * a.ndim)
    n_vt = DIFF_HEADS * VT_ROWS
    grid_spec = pltpu.PrefetchScalarGridSpec(
        num_scalar_prefetch=1,
        grid=(B, S // tq, S // tk),
        in_specs=[qcol(256), krow(512), kcol(n_vt), qcol(512), krow(512), kcol(n_vt),
                  qcol(1), krow(1), full(lam), full(hg), full(linit)],
        out_specs=[pl.BlockSpec((1, tq, 256), lambda b, i, k, f: (b, i, 0))] * 2,
        scratch_shapes=[pltpu.VMEM((N_MAPS, LANE, tq), BF16),
                        pltpu.VMEM((16, tq), F32), pltpu.VMEM((16, tq), F32),
                        pltpu.VMEM((N_MAPS, VT_ROWS, tq), F32),
                        pltpu.VMEM((SLOW_DEPTH + 2, tk, STRIP), F32)],
    )
    return pl.pallas_call(
        _dense_kernel,
        grid_spec=grid_spec,
        out_shape=[jax.ShapeDtypeStruct((B, S, 256), BF16)] * 2,
        compiler_params=_params("parallel", "parallel", "arbitrary"),
        name="dense_attn",
    )(fast, p["qtd"], p["kd"], p["vtd"], p["qtm"], p["km"], p["vtm"], posq, posk, lam, hg, linit)


SWA_DEPTH = 5


def _swa_kernel(qt_ref, kc_ref, kp_ref, kn_ref, vc_ref, vp_ref, vn_ref,
                posq_ref, pkc_ref, pkp_ref, pkn_ref, sink_ref, o_ref):
    i = pl.program_id(1)
    last = pl.num_programs(1) - 1
    tq = qt_ref.shape[2]
    w = SWA_WINDOW
    n_ext = tq + 2 * w
    k_ext = jnp.concatenate([kp_ref[0], kc_ref[0], kn_ref[0]], axis=0)
    vt_ext = jnp.concatenate([vp_ref[0], vc_ref[0], vn_ref[0]], axis=1)
    pos_ext = jnp.concatenate([pkp_ref[0], pkc_ref[0], pkn_ref[0]], axis=0)
    row = lax.broadcasted_iota(jnp.int32, (n_ext, 1), 0)
    far = 4 * n_ext
    kidx = (row - w + jnp.where(row < w, jnp.where(i == 0, far, 0), 0)
            + jnp.where(row >= tq + w, jnp.where(i == last, far, 0), 0))
    zeros = jnp.zeros((HEAD_DIM, tq), BF16)
    wqs = []
    for hq in range(SWA_Q_HEADS):
        q_h = qt_ref[0, hq * HEAD_DIM:(hq + 1) * HEAD_DIM, :]
        wqs.append(jnp.concatenate([q_h, zeros] if hq // SWA_GROUP == 0 else [zeros, q_h], axis=0))

    def item_logits(r, hq, dist, ok):
        sink = sink_ref[hq:hq + 1, :] * LOG2E
        s = _dot(k_ext[r * w:(r + 3) * w], wqs[hq][:, r * w:(r + 1) * w]) - dist * (SWA_SLOPES[hq] * LOG2E)
        s = jnp.where(ok, s, NEG_BIG)
        m = jnp.maximum(sink, jnp.max(s, axis=0, keepdims=True))
        return s, m, sink

    def item_output(r, hq, s, m, sink):
        kv = hq // SWA_GROUP
        e = jnp.exp2(s - m)
        den = jnp.exp2(sink - m) + jnp.sum(e, axis=0, keepdims=True)
        return _dot(vt_ext[kv * HEAD_DIM:(kv + 1) * HEAD_DIM, r * w:(r + 3) * w], e.astype(BF16)) / den

    items = []
    for r in range(tq // w):
        dist = jnp.abs(pos_ext[r * w:(r + 3) * w] - posq_ref[0, :, r * w:(r + 1) * w]).astype(F32)
        qidx = lax.broadcasted_iota(jnp.int32, (1, w), 1) + r * w
        ok = jnp.abs(kidx[r * w:(r + 3) * w] - qidx) <= w
        items += [(r, hq, dist, ok) for hq in range(SWA_Q_HEADS)]
    outs = {}
    ahead = {}
    for t in range(len(items) + SWA_DEPTH):
        if t < len(items):
            ahead[t] = item_logits(*items[t])
        if t >= SWA_DEPTH:
            r, hq, _, _ = items[t - SWA_DEPTH]
            outs[(hq, r)] = item_output(r, hq, *ahead.pop(t - SWA_DEPTH))
    heads = [jnp.concatenate([outs[(hq, r)] for r in range(tq // w)], axis=1) for hq in range(SWA_Q_HEADS)]
    o_ref[0] = jnp.concatenate(heads, axis=0).T.astype(BF16)


def _swa_call(p, posq, posk, sinks, tq):
    B, _, S = p["qts"].shape
    w = SWA_WINDOW
    r = tq // w
    nblk = S // w
    prev = lambda b, i: (b, jnp.maximum(i * r - 1, 0), 0)
    nxt = lambda b, i: (b, jnp.minimum((i + 1) * r, nblk - 1), 0)
    prev_t = lambda b, i: (b, 0, jnp.maximum(i * r - 1, 0))
    nxt_t = lambda b, i: (b, 0, jnp.minimum((i + 1) * r, nblk - 1))
    cur = lambda b, i: (b, i, 0)
    cur_t = lambda b, i: (b, 0, i)
    return pl.pallas_call(
        _swa_kernel,
        grid=(B, S // tq),
        in_specs=[pl.BlockSpec((1, 512, tq), cur_t),
                  pl.BlockSpec((1, tq, 128), cur), pl.BlockSpec((1, w, 128), prev), pl.BlockSpec((1, w, 128), nxt),
                  pl.BlockSpec((1, 128, tq), cur_t), pl.BlockSpec((1, 128, w), prev_t),
                  pl.BlockSpec((1, 128, w), nxt_t),
                  pl.BlockSpec((1, 1, tq), cur_t),
                  pl.BlockSpec((1, tq, 1), cur), pl.BlockSpec((1, w, 1), prev), pl.BlockSpec((1, w, 1), nxt),
                  pl.BlockSpec(sinks.shape, lambda b, i: (0, 0))],
        out_specs=pl.BlockSpec((1, tq, 512), cur),
        out_shape=jax.ShapeDtypeStruct((B, S, 512), BF16),
        compiler_params=_params("parallel", "parallel"),
        name="swa_attn",
    )(p["qts"], p["ks"], p["ks"], p["ks"], p["vts"], p["vts"], p["vts"],
      posq, posk, posk, posk, sinks)


def _memkv_kernel(mem_ref, g_ref, w_ref, kv_ref):
    mem_n = _rms_rows(mem_ref[0], g_ref[...]).astype(BF16)
    kv_ref[0] = _dot(mem_n, w_ref[...]).astype(BF16)


def _memkv_call(mem, g, w):
    B, M, _ = mem.shape
    return pl.pallas_call(
        _memkv_kernel,
        grid=(B,),
        in_specs=[pl.BlockSpec((1, M, D_MODEL), lambda b: (b, 0, 0)),
                  pl.BlockSpec(g.shape, lambda b: (0, 0)), pl.BlockSpec(w.shape, lambda b: (0, 0))],
        out_specs=pl.BlockSpec((1, M, 2 * D_MODEL), lambda b: (b, 0, 0)),
        out_shape=jax.ShapeDtypeStruct((B, M, 2 * D_MODEL), BF16),
        compiler_params=_params("parallel"),
        name="mem_kv",
    )(mem, g, w)


def _post_kernel(x_ref, od_ref, om_ref, os_ref, wout_ref, gmp_ref, gxp_ref, wxq_ref, kv_ref, wxo_ref,
                 gxo_ref, o_ref):
    mix = (_dot(od_ref[0], wout_ref[0:256, :]) + _dot(om_ref[0], wout_ref[256:512, :])
           + _dot(os_ref[0], wout_ref[512:1024, :]))
    x1 = x_ref[0] + _rms_rows(mix, gmp_ref[...])
    hq = _rms_rows(x1, gxp_ref[...]).astype(BF16)
    q = (_dot(hq, wxq_ref[...]) * (LOG2E / math.sqrt(X_HEAD_DIM))).astype(BF16)
    heads = []
    for hd in range(X_HEADS):
        c0 = hd * X_HEAD_DIM
        s = _nt_dot(q[:, c0:c0 + X_HEAD_DIM], kv_ref[0, :, c0:c0 + X_HEAD_DIM])
        e = jnp.exp2(s - jnp.max(s, axis=-1, keepdims=True))
        den = jnp.sum(e, axis=-1, keepdims=True)
        o = _dot(e.astype(BF16), kv_ref[0, :, D_MODEL + c0:D_MODEL + c0 + X_HEAD_DIM])
        heads.append((o / den).astype(BF16))
    xo = _dot(jnp.concatenate(heads, axis=-1), wxo_ref[...])
    o_ref[0] = x1 + _rms_rows(xo, gxo_ref[...])


def _post_call(x, od, om, os_, wout, gmp, gxp, wxq, kv, wxo, gxo, ts):
    B, S, _ = x.shape
    row = lambda n: pl.BlockSpec((1, ts, n), lambda b, i: (b, i, 0))
    full = lambda a: pl.BlockSpec(a.shape, lambda b, i: (0,) * a.ndim)
    kv_spec = pl.BlockSpec((1,) + kv.shape[1:], lambda b, i: (b, 0, 0))
    return pl.pallas_call(
        _post_kernel,
        grid=(B, S // ts),
        in_specs=[row(D_MODEL), row(256), row(256), row(512), full(wout), full(gmp), full(gxp), full(wxq),
                  kv_spec, full(wxo), full(gxo)],
        out_specs=row(D_MODEL),
        out_shape=jax.ShapeDtypeStruct(x.shape, F32),
        compiler_params=_params("parallel", "parallel"),
        name="mix_cross",
    )(x, od, om, os_, wout, gmp, gxp, wxq, kv, wxo, gxo)


def _ffn_kernel(x_ref, gpre_ref, win_ref, wout_ref, gpost_ref, o_ref):
    x = x_ref[0]
    h = _rms_rows(x, gpre_ref[...]).astype(BF16)
    gate = _dot(h, win_ref[:, 0:D_FF])
    up = _dot(h, win_ref[:, D_FF:2 * D_FF])
    f = _dot((gate * jax.nn.sigmoid(gate) * up).astype(BF16), wout_ref[...])
    o_ref[0] = x + _rms_rows(f, gpost_ref[...])


def _ffn_call(x, gpre, win, wout, gpost, ts):
    B, S, _ = x.shape
    row = pl.BlockSpec((1, ts, D_MODEL), lambda b, i: (b, i, 0))
    full = lambda a: pl.BlockSpec(a.shape, lambda b, i: (0,) * a.ndim)
    return pl.pallas_call(
        _ffn_kernel,
        grid=(B, S // ts),
        in_specs=[row, full(gpre), full(win), full(wout), full(gpost)],
        out_specs=row,
        out_shape=jax.ShapeDtypeStruct(x.shape, F32),
        compiler_params=_params("parallel", "parallel"),
        name="ffn",
    )(x, gpre, win, wout, gpost)


def _prep_weights(w_in, mla_q_norm_g, mla_w_q_up, mla_kv_norm_g, mla_w_kv_up):
    swap = np.concatenate([np.arange(16, 32), np.arange(0, 16)])
    kr = w_in[:, O_BKR:O_CQ]
    zeros64 = jnp.zeros((D_MODEL, 64), w_in.dtype)
    ak = w_in[:, O_AK:O_AV]
    ak_slabs = [a for hd in range(DIFF_HEADS) for a in (ak[:, 64 * hd:64 * (hd + 1)], zeros64)]
    wnat = jnp.concatenate(
        ak_slabs + [w_in[:, O_BCQ:O_BCKV], w_in[:, O_BCKV:O_BKR], kr, kr[:, swap], zeros64,
                    w_in[:, O_CK:O_CV]], axis=1)
    wtr = jnp.concatenate(
        [w_in[:, O_AQ:O_AK], w_in[:, O_AV:O_BCQ], w_in[:, O_CQ:O_CK], w_in[:, O_CV:O_END]], axis=1).T
    qup = mla_w_q_up.reshape(MLA_Q_RANK, MLA_HEADS, MLA_NOPE + MLA_ROPE)
    wqa = jnp.concatenate([qup, jnp.zeros((MLA_Q_RANK, MLA_HEADS, 32), qup.dtype)], axis=2)
    wqa = wqa.reshape(MLA_Q_RANK, MLA_HEADS * LANE).T
    wqb = qup[:, :, MLA_NOPE:][:, :, swap].reshape(MLA_Q_RANK, MLA_HEADS * MLA_ROPE).T
    kvup = mla_w_kv_up.reshape(MLA_KV_RANK, MLA_HEADS, MLA_NOPE + MLA_V)
    wk = jnp.concatenate([kvup[:, :, :MLA_NOPE], jnp.zeros((MLA_KV_RANK, MLA_HEADS, 64), kvup.dtype)], axis=2)
    wk = wk.reshape(MLA_KV_RANK, MLA_HEADS * LANE)
    wvt = kvup[:, :, MLA_NOPE:].reshape(MLA_KV_RANK, MLA_HEADS * MLA_V).T
    place = np.zeros((LANE, MLA_HEADS * LANE), np.float32)
    group_d = np.zeros((DIFF_HEADS * LANE, LANE), np.float32)
    group_m = np.zeros((MLA_HEADS * LANE, LANE), np.float32)
    for hd in range(MLA_HEADS):
        place[np.arange(32), hd * LANE + MLA_NOPE + np.arange(32)] = 1.0
        group_m[hd * LANE + np.arange(MLA_NOPE + MLA_ROPE), DIFF_MAPS + hd] = 1.0
    for mp in range(DIFF_MAPS):
        group_d[(mp // 2) * LANE + (mp % 2) * DIFF_QK + np.arange(DIFF_QK), mp] = 1.0
    return {
        "group_d": jnp.asarray(group_d, BF16), "group_m": jnp.asarray(group_m, BF16),
        "wnat": wnat.astype(BF16), "wtr": wtr.astype(BF16),
        "gq": mla_q_norm_g.reshape(1, -1), "gkv": mla_kv_norm_g.reshape(1, -1),
        "wqa": wqa.astype(BF16), "wqb": wqb.astype(BF16), "wk": wk.astype(BF16), "wvt": wvt.astype(BF16),
        "place": jnp.asarray(place, BF16),
    }


def _rope_tables(positions):
    half = MLA_ROPE // 2
    inv = ROPE_THETA ** (-jnp.arange(half, dtype=F32) / half)
    ang = positions.astype(F32)[..., None] * inv
    cos, sin = jnp.cos(ang), jnp.sin(ang)
    cos2 = jnp.concatenate([cos, cos], axis=-1)
    sin2 = jnp.concatenate([-sin, sin], axis=-1)
    tab = jnp.concatenate([cos2, sin2, jnp.zeros(cos2.shape[:2] + (64,), F32)], axis=-1)
    return {"tab": tab, "cos_t": cos2.swapaxes(1, 2), "sin_t": sin2.swapaxes(1, 2)}


def _fast_tiles(qstat, kstat, nq, nk):
    B = qstat.shape[0]
    qn = qstat[:, :, :N_MAPS, 0].reshape(B, nq, -1, N_MAPS).max(axis=2)
    kn = kstat[:, :, 0, :N_MAPS].reshape(B, nk, -1, N_MAPS).max(axis=2)
    bound2 = qn[:, :, None, :] * kn[:, None, :, :]
    margin = 0.97
    return jnp.all(bound2 <= (margin * LOGIT_LIMIT) ** 2, axis=-1).astype(jnp.int32)


def _tile(n, want):
    t = min(n, want)
    assert n % t == 0, (n, t)
    return t


def kernel(x, mem, positions, g_mix_pre, g_mix_post, w_in, diff_lambda, diff_head_g, mla_q_norm_g, mla_w_q_up,
           mla_kv_norm_g, mla_w_kv_up, swa_sinks, w_out, g_x_pre, g_x_mem, g_x_post, w_xq, w_xkv, w_xo,
           g_ffn_pre, g_ffn_post, w_ffn_in, w_ffn_out):
    B, S, _ = x.shape
    depth = w_in.shape[0]
    ts_proj, ts_post, ts_ffn = _tile(S, 512), _tile(S, 512), _tile(S, 256)
    tq, tk, tq_swa = _tile(S, 2048), _tile(S, 512), _tile(S, 512)

    tabs = _rope_tables(positions)
    posq = positions.reshape(B, 1, S)
    posk = positions.reshape(B, S, 1)
    row = lambda v: v.reshape(1, -1)

    for l in range(depth):
        w = _prep_weights(w_in[l], mla_q_norm_g[l], mla_w_q_up[l], mla_kv_norm_g[l], mla_w_kv_up[l])
        names = ("qtd", "kd", "vtd", "qtm", "km", "vtm", "qts", "ks", "vts", "qstat", "kstat")
        p = dict(zip(names, _proj_call(x, row(g_mix_pre[l]), w, tabs, ts_proj)))
        linit = jnp.full((1, 1), 0.8 - 0.6 * math.exp(-0.3 * l), F32)
        fast = _fast_tiles(p["qstat"], p["kstat"], S // tq, S // tk)
        od, om = _dense_call(p, fast, posq, posk, diff_lambda[l], diff_head_g[l].reshape(-1, 1), linit, tq, tk)
        os_ = _swa_call(p, posq, posk, swa_sinks[l].reshape(-1, 1), tq_swa)
        kv = _memkv_call(mem, row(g_x_mem[l]), w_xkv[l].astype(BF16))
        x = _post_call(x, od, om, os_, w_out[l].astype(BF16), row(g_mix_post[l]), row(g_x_pre[l]),
                       w_xq[l].astype(BF16), kv, w_xo[l].astype(BF16), row(g_x_post[l]), ts_post)
        x = _ffn_call(x, row(g_ffn_pre[l]), w_ffn_in[l].astype(BF16), w_ffn_out[l].astype(BF16),
                      row(g_ffn_post[l]), ts_ffn)
    return x
```

```python
import math

import jax
import jax.numpy as jnp
import numpy as np
from jax import lax
from jax.experimental import pallas as pl
from jax.experimental.pallas import tpu as pltpu

F32 = jnp.float32
BF16 = jnp.bfloat16

D_MODEL = 1024
HEAD_DIM = 64
DIFF_HEADS = 4
DIFF_QK = 32
DIFF_MAPS = 2 * DIFF_HEADS
MLA_HEADS = 4
MLA_Q_RANK = 256
MLA_KV_RANK = 128
MLA_NOPE = 64
MLA_ROPE = 32
MLA_V = 64
SWA_Q_HEADS = 8
SWA_KV_HEADS = 2
SWA_GROUP = SWA_Q_HEADS // SWA_KV_HEADS
SWA_WINDOW = 128
X_HEADS = 4
X_HEAD_DIM = D_MODEL // X_HEADS
D_FF = 2816
N_ALIBI = DIFF_HEADS + SWA_Q_HEADS
ROPE_THETA = 10000.0
EPS = 1e-6
LOG2E = math.log2(math.e)
NEG_BIG = -1e30

LANE = 128
VMEM_LIMIT = 56 * 1024 * 1024

_OFF = np.cumsum([0, 256, 256, 256, MLA_Q_RANK, MLA_KV_RANK, MLA_ROPE, 512, 128, 128])
(O_AQ, O_AK, O_AV, O_BCQ, O_BCKV, O_BKR, O_CQ, O_CK, O_CV, O_END) = [int(v) for v in _OFF]

ONES_ROWS = 16
VT_ROWS = 64 + ONES_ROWS
AUG_LANES = 3
DIFF_AUG = 64
MLA_AUG = 96
LOGIT_LIMIT = 40.0


def _alibi_slopes():
    return [2.0 ** (-8.0 * i / N_ALIBI) for i in range(1, N_ALIBI + 1)]


SWA_SLOPES = _alibi_slopes()[:SWA_Q_HEADS]
DIFF_SLOPES = _alibi_slopes()[SWA_Q_HEADS:]


def _nt_dot(a, b):
    return lax.dot_general(a, b, (((1,), (1,)), ((), ())), preferred_element_type=F32)


def _dot(a, b):
    return jnp.dot(a, b, preferred_element_type=F32)


def _rms_rows(x, g):
    return x * lax.rsqrt(jnp.mean(x * x, axis=-1, keepdims=True) + EPS) * g


def _params(*sem, flags=None):
    return pltpu.CompilerParams(dimension_semantics=sem, vmem_limit_bytes=VMEM_LIMIT, flags=flags)


def _store_values_t(vt_ref, vt, n_heads):
    ones = jnp.ones((ONES_ROWS, vt.shape[1]), BF16)
    for hd in range(n_heads):
        vt_ref[0, hd * VT_ROWS:hd * VT_ROWS + 64, :] = vt[hd * 64:(hd + 1) * 64].astype(BF16)
        vt_ref[0, hd * VT_ROWS + 64:(hd + 1) * VT_ROWS, :] = ones


def _ones_lanes(width, first):
    lane = lax.broadcasted_iota(jnp.int32, (1, width), 1) % LANE
    return jnp.where((lane >= first) & (lane < first + AUG_LANES), 1.0, 0.0)


def _proj_kernel(x_ref, g_ref, wnat_ref, wtr_ref, gq_ref, gkv_ref, wqa_ref, wqb_ref, wk_ref, wvt_ref,
                 pl_ref, gd_ref, gm_ref, tab_ref, cost_ref, sint_ref,
                 qtd_ref, kd_ref, vtd_ref, qtm_ref, km_ref, vtm_ref, qts_ref, ks_ref, vts_ref,
                 qst_ref, kst_ref):
    h = _rms_rows(x_ref[0], g_ref[...]).astype(BF16)
    nat = _dot(h, wnat_ref[...])
    tr = _nt_dot(wtr_ref[...], h)
    ts = nat.shape[0]

    c_diff = LOG2E / math.sqrt(DIFF_QK)
    c_mla = LOG2E / math.sqrt(MLA_NOPE + MLA_ROPE)
    c_swa = LOG2E / math.sqrt(HEAD_DIM)

    qd = tr[0:256] * c_diff
    qtd_ref[0] = qd.astype(BF16)
    _store_values_t(vtd_ref, tr[256:512], DIFF_HEADS)
    qts_ref[0] = (tr[512:1024] * c_swa).astype(BF16)
    vts_ref[0] = tr[1024:1152].astype(BF16)
    kd = nat[:, 0:512]
    kd_ref[0] = (kd + _ones_lanes(512, DIFF_AUG)).astype(BF16)
    ks_ref[0] = nat[:, 1024:1152].astype(BF16)

    cq_n = _rms_rows(nat[:, 512:768], gq_ref[...]).astype(BF16)
    ckv_n = _rms_rows(nat[:, 768:896], gkv_ref[...]).astype(BF16)

    u = nat[:, 896:1024] * tab_ref[0]
    kr = (u + pltpu.roll(u, 96, 1)).astype(BF16)
    km = _dot(ckv_n, wk_ref[...]) + _dot(kr, pl_ref[...])
    km_ref[0] = (km + _ones_lanes(512, MLA_AUG)).astype(BF16)
    _store_values_t(vtm_ref, _nt_dot(wvt_ref[...], ckv_n), MLA_HEADS)

    qa = _nt_dot(wqa_ref[...], cq_n)
    qb = _nt_dot(wqb_ref[...], cq_n)
    cos_t = cost_ref[0]
    sin_t = sint_ref[0]
    for hd in range(MLA_HEADS):
        r0 = hd * LANE
        qtm_ref[0, r0:r0 + 64, :] = (qa[r0:r0 + 64] * c_mla).astype(BF16)
        rope = qa[r0 + 64:r0 + 96] * cos_t + qb[hd * 32:hd * 32 + 32] * sin_t
        qtm_ref[0, r0 + 64:r0 + 96, :] = (rope * c_mla).astype(BF16)
        qtm_ref[0, r0 + 96:r0 + 128, :] = jnp.zeros((32, rope.shape[1]), BF16)

    qn = [jnp.sum((qd * qd).reshape(DIFF_MAPS, DIFF_QK, ts), axis=1)]
    qm = (qa * qa).reshape(MLA_HEADS, LANE, ts)
    qn.append(jnp.sum(qm, axis=1) * (c_mla * c_mla))
    qn.append(jnp.zeros((16 - DIFF_MAPS - MLA_HEADS, ts), F32))
    qmax = jnp.max(jnp.concatenate(qn, axis=0), axis=1, keepdims=True)
    qst_ref[0, 0] = jnp.broadcast_to(qmax, (16, LANE))
    kn = _dot((kd * kd).astype(BF16), gd_ref[...]) + _dot((km * km).astype(BF16), gm_ref[...])
    kst_ref[0, 0] = jnp.broadcast_to(jnp.max(kn, axis=0, keepdims=True), (8, LANE))


def _proj_call(x, g, w, tabs, ts):
    B, S, _ = x.shape
    row = lambda n: pl.BlockSpec((1, ts, n), lambda b, i: (b, i, 0))
    col = lambda n: pl.BlockSpec((1, n, ts), lambda b, i: (b, 0, i))
    full = lambda a: pl.BlockSpec(a.shape, lambda b, i: (0,) * a.ndim)
    consts = [g, w["wnat"], w["wtr"], w["gq"], w["gkv"], w["wqa"], w["wqb"], w["wk"], w["wvt"], w["place"],
              w["group_d"], w["group_m"]]
    n_vt = DIFF_HEADS * VT_ROWS
    nt = S // ts
    stat = lambda r: pl.BlockSpec((1, 1, r, LANE), lambda b, i: (b, i, 0, 0))
    out_shape = [
        jax.ShapeDtypeStruct((B, 256, S), BF16), jax.ShapeDtypeStruct((B, S, 512), BF16),
        jax.ShapeDtypeStruct((B, n_vt, S), BF16), jax.ShapeDtypeStruct((B, 512, S), BF16),
        jax.ShapeDtypeStruct((B, S, 512), BF16), jax.ShapeDtypeStruct((B, n_vt, S), BF16),
        jax.ShapeDtypeStruct((B, 512, S), BF16), jax.ShapeDtypeStruct((B, S, 128), BF16),
        jax.ShapeDtypeStruct((B, 128, S), BF16),
        jax.ShapeDtypeStruct((B, nt, 16, LANE), F32), jax.ShapeDtypeStruct((B, nt, 8, LANE), F32),
    ]
    out_specs = [col(256), row(512), col(n_vt), col(512), row(512), col(n_vt), col(512), row(128), col(128),
                 stat(16), stat(8)]
    return pl.pallas_call(
        _proj_kernel,
        grid=(B, S // ts),
        in_specs=[row(D_MODEL)] + [full(a) for a in consts] + [row(128), col(32), col(32)],
        out_specs=out_specs,
        out_shape=out_shape,
        compiler_params=_params("parallel", "parallel"),
        name="in_proj",
    )(x, *consts, tabs["tab"], tabs["cos_t"], tabs["sin_t"])


STRIP = 256
SLOW_DEPTH = 3
FAST_DEPTH = 2
FAST_KEYS = 512
N_MAPS = DIFF_MAPS + MLA_HEADS


def _split3(x):
    hi = x.astype(BF16).astype(F32)
    mid = (x - hi).astype(BF16).astype(F32)
    lo = (x - hi - mid).astype(BF16).astype(F32)
    return hi, mid, lo


def _dense_kernel(fast_ref, qtd_ref, kd_ref, vtd_ref, qtm_ref, km_ref, vtm_ref, posq_ref, posk_ref,
                  lam_ref, hg_ref, linit_ref, od_ref, om_ref,
                  wq_ref, m_ref, alpha_ref, acc_ref, s_ref):
    b, qi, ki = pl.program_id(0), pl.program_id(1), pl.program_id(2)
    tq = qtd_ref.shape[2]
    n_slots = s_ref.shape[0]

    @pl.when(ki == 0)
    def _init():
        m_ref[...] = jnp.full(m_ref.shape, NEG_BIG, F32)
        acc_ref[...] = jnp.zeros(acc_ref.shape, F32)
        for mp in range(DIFF_MAPS):
            q = qtd_ref[0, mp * DIFF_QK:(mp + 1) * DIFF_QK, :]
            above = jnp.zeros(((mp % 2) * DIFF_QK, tq), BF16)
            below = jnp.zeros((LANE - (mp % 2 + 1) * DIFF_QK, tq), BF16)
            wq_ref[mp] = jnp.concatenate([above, q, below] if mp % 2 else [q, below], axis=0)
        for hd in range(MLA_HEADS):
            wq_ref[DIFF_MAPS + hd] = qtm_ref[0, hd * LANE:(hd + 1) * LANE, :]

    def key_slab(mp, keys):
        if mp < DIFF_MAPS:
            return kd_ref[0, keys, (mp // 2) * LANE:(mp // 2 + 1) * LANE]
        hd = mp - DIFF_MAPS
        return km_ref[0, keys, hd * LANE:(hd + 1) * LANE]

    def values_t(mp, keys):
        if mp < DIFF_MAPS:
            return vtd_ref[0, (mp // 2) * VT_ROWS:(mp // 2 + 1) * VT_ROWS, keys]
        hd = mp - DIFF_MAPS
        return vtm_ref[0, hd * VT_ROWS:(hd + 1) * VT_ROWS, keys]

    def set_shift_rows(shift):
        rows = lax.broadcasted_iota(jnp.int32, (ONES_ROWS, tq), 0)
        for mp in range(N_MAPS):
            hi, mid, lo = _split3(shift[mp:mp + 1, :])
            block = jnp.where(rows == 0, hi, jnp.where(rows == 1, mid, jnp.where(rows == 2, lo, 0.0)))
            first = DIFF_AUG if mp < DIFF_MAPS else MLA_AUG
            wq_ref[mp, first:first + ONES_ROWS, :] = block.astype(BF16)

    def work_items(chunk):
        items = []
        for st in range(tq // STRIP):
            lanes = slice(st * STRIP, (st + 1) * STRIP)
            for kc in range(posk_ref.shape[1] // chunk):
                keys = slice(kc * chunk, (kc + 1) * chunk)
                dist = jnp.abs(posk_ref[0, keys] - posq_ref[0, :, lanes]).astype(F32)
                bias = [dist * (slope * LOG2E) for slope in DIFF_SLOPES]
                items += [(mp, lanes, keys, bias) for mp in range(N_MAPS)]
        return items

    def logits(mp, lanes, keys, bias):
        s = _dot(key_slab(mp, keys), wq_ref[mp, :, lanes])
        return s - bias[mp // 2] if mp < DIFF_MAPS else s

    fast = fast_ref[b, qi, ki] != 0

    @pl.when(fast)
    def _fast_step():
        m_old = m_ref[...]
        dmin = jnp.min(jnp.abs(posk_ref[0] - posq_ref[0]), axis=0, keepdims=True).astype(F32)
        rows = lax.broadcasted_iota(jnp.int32, m_old.shape, 0)
        slope = jnp.zeros(m_old.shape, F32)
        for hd in range(DIFF_HEADS):
            slope = jnp.where(rows // 2 == hd, DIFF_SLOPES[hd] * LOG2E, slope)
        m_eff = jnp.maximum(m_old, -LOGIT_LIMIT - slope * dmin)
        alpha_ref[...] = jnp.exp2(m_old - m_eff)
        m_ref[...] = m_eff
        set_shift_rows(-m_eff)
        items = work_items(FAST_KEYS)
        vals = {}
        for t in range(len(items) + FAST_DEPTH):
            if t < len(items):
                vals[t] = logits(*items[t])
            if t >= FAST_DEPTH:
                mp, lanes, keys, _ = items[t - FAST_DEPTH]
                p = jnp.exp2(vals.pop(t - FAST_DEPTH)).astype(BF16)
                old = acc_ref[mp, :, lanes]
                if keys.start == 0:
                    old = alpha_ref[mp:mp + 1, lanes] * old
                acc_ref[mp, :, lanes] = old + _dot(values_t(mp, keys), p)

    @pl.when(jnp.logical_not(fast))
    def _slow_step():
        set_shift_rows(jnp.zeros(m_ref.shape, F32))
        items = work_items(posk_ref.shape[1])

        def stage_qk(j):
            s_ref[j % n_slots] = logits(*items[j])

        def stage_max(j):
            mp, lanes, _, _ = items[j]
            m_old = m_ref[mp:mp + 1, lanes]
            m_new = jnp.maximum(m_old, jnp.max(s_ref[j % n_slots], axis=0, keepdims=True))
            m_ref[mp:mp + 1, lanes] = m_new
            return m_new, jnp.exp2(m_old - m_new)

        def stage_pv(j, m_new, alpha):
            mp, lanes, keys, _ = items[j]
            p = jnp.exp2((s_ref[j % n_slots] - m_new).astype(BF16))
            acc_ref[mp, :, lanes] = alpha * acc_ref[mp, :, lanes] + _dot(values_t(mp, keys), p)

        stats = {}
        for t in range(len(items) + SLOW_DEPTH):
            if t < len(items):
                stage_qk(t)
            if 1 <= t <= len(items):
                stats[t - 1] = stage_max(t - 1)
            if t >= SLOW_DEPTH:
                stage_pv(t - SLOW_DEPTH, *stats.pop(t - SLOW_DEPTH))

    @pl.when(ki == pl.num_programs(2) - 1)
    def _finish():
        lp = lam_ref[...]
        lam_init = linit_ref[...]
        lam = (jnp.exp(jnp.sum(lp[0:1] * lp[1:2], axis=-1, keepdims=True))
               - jnp.exp(jnp.sum(lp[2:3] * lp[3:4], axis=-1, keepdims=True)) + lam_init)
        def normalised(idx):
            return acc_ref[idx, 0:64, :] / acc_ref[idx, 64:65, :]

        outs = []
        for hd in range(DIFF_HEADS):
            o = normalised(2 * hd) - lam * normalised(2 * hd + 1)
            o = o * lax.rsqrt(jnp.mean(o * o, axis=0, keepdims=True) + EPS) * hg_ref[...]
            outs.append(o * (1.0 - lam_init))
        od_ref[0] = jnp.concatenate(outs, axis=0).T.astype(BF16)
        outs = [normalised(DIFF_MAPS + hd) for hd in range(MLA_HEADS)]
        om_ref[0] = jnp.concatenate(outs, axis=0).T.astype(BF16)


def _dense_call(p, fast, posq, posk, lam, hg, linit, tq, tk):
    B, _, S = p["qtd"].shape
    qcol = lambda n: pl.BlockSpec((1, n, tq), lambda b, i, k, f: (b, 0, i))
    krow = lambda n: pl.BlockSpec((1, tk, n), lambda b, i, k, f: (b, k, 0))
    kcol = lambda n: pl.BlockSpec((1, n, tk), lambda b, i, k, f: (b, 0, k))
    full = lambda a: pl.BlockSpec(a.shape, lambda b, i, k, f: (0,) * a.ndim)
    n_vt = DIFF_HEADS * VT_ROWS
    grid_spec = pltpu.PrefetchScalarGridSpec(
        num_scalar_prefetch=1,
        grid=(B, S // tq, S // tk),
        in_specs=[qcol(256), krow(512), kcol(n_vt), qcol(512), krow(512), kcol(n_vt),
                  qcol(1), krow(1), full(lam), full(hg), full(linit)],
        out_specs=[pl.BlockSpec((1, tq, 256), lambda b, i, k, f: (b, i, 0))] * 2,
        scratch_shapes=[pltpu.VMEM((N_MAPS, LANE, tq), BF16),
                        pltpu.VMEM((16, tq), F32), pltpu.VMEM((16, tq), F32),
                        pltpu.VMEM((N_MAPS, VT_ROWS, tq), F32),
                        pltpu.VMEM((SLOW_DEPTH + 2, tk, STRIP), F32)],
    )
    return pl.pallas_call(
        _dense_kernel,
        grid_spec=grid_spec,
        out_shape=[jax.ShapeDtypeStruct((B, S, 256), BF16)] * 2,
        compiler_params=_params("parallel", "parallel", "arbitrary"),
        name="dense_attn",
    )(fast, p["qtd"], p["kd"], p["vtd"], p["qtm"], p["km"], p["vtm"], posq, posk, lam, hg, linit)


SWA_DEPTH = 5


def _swa_kernel(qt_ref, kc_ref, kp_ref, kn_ref, vc_ref, vp_ref, vn_ref,
                posq_ref, pkc_ref, pkp_ref, pkn_ref, sink_ref, o_ref):
    i = pl.program_id(1)
    last = pl.num_programs(1) - 1
    tq = qt_ref.shape[2]
    w = SWA_WINDOW
    n_ext = tq + 2 * w
    k_ext = jnp.concatenate([kp_ref[0], kc_ref[0], kn_ref[0]], axis=0)
    vt_ext = jnp.concatenate([vp_ref[0], vc_ref[0], vn_ref[0]], axis=1)
    pos_ext = jnp.concatenate([pkp_ref[0], pkc_ref[0], pkn_ref[0]], axis=0)
    row = lax.broadcasted_iota(jnp.int32, (n_ext, 1), 0)
    far = 4 * n_ext
    kidx = (row - w + jnp.where(row < w, jnp.where(i == 0, far, 0), 0)
            + jnp.where(row >= tq + w, jnp.where(i == last, far, 0), 0))
    zeros = jnp.zeros((HEAD_DIM, tq), BF16)
    wqs = []
    for hq in range(SWA_Q_HEADS):
        q_h = qt_ref[0, hq * HEAD_DIM:(hq + 1) * HEAD_DIM, :]
        wqs.append(jnp.concatenate([q_h, zeros] if hq // SWA_GROUP == 0 else [zeros, q_h], axis=0))

    def item_logits(r, hq, dist, ok):
        sink = sink_ref[hq:hq + 1, :] * LOG2E
        s = _dot(k_ext[r * w:(r + 3) * w], wqs[hq][:, r * w:(r + 1) * w]) - dist * (SWA_SLOPES[hq] * LOG2E)
        s = jnp.where(ok, s, NEG_BIG)
        m = jnp.maximum(sink, jnp.max(s, axis=0, keepdims=True))
        return s, m, sink

    def item_output(r, hq, s, m, sink):
        kv = hq // SWA_GROUP
        e = jnp.exp2(s - m)
        den = jnp.exp2(sink - m) + jnp.sum(e, axis=0, keepdims=True)
        return _dot(vt_ext[kv * HEAD_DIM:(kv + 1) * HEAD_DIM, r * w:(r + 3) * w], e.astype(BF16)) / den

    items = []
    for r in range(tq // w):
        dist = jnp.abs(pos_ext[r * w:(r + 3) * w] - posq_ref[0, :, r * w:(r + 1) * w]).astype(F32)
        qidx = lax.broadcasted_iota(jnp.int32, (1, w), 1) + r * w
        ok = jnp.abs(kidx[r * w:(r + 3) * w] - qidx) <= w
        items += [(r, hq, dist, ok) for hq in range(SWA_Q_HEADS)]
    outs = {}
    ahead = {}
    for t in range(len(items) + SWA_DEPTH):
        if t < len(items):
            ahead[t] = item_logits(*items[t])
        if t >= SWA_DEPTH:
            r, hq, _, _ = items[t - SWA_DEPTH]
            outs[(hq, r)] = item_output(r, hq, *ahead.pop(t - SWA_DEPTH))
    heads = [jnp.concatenate([outs[(hq, r)] for r in range(tq // w)], axis=1) for hq in range(SWA_Q_HEADS)]
    o_ref[0] = jnp.concatenate(heads, axis=0).T.astype(BF16)


def _swa_call(p, posq, posk, sinks, tq):
    B, _, S = p["qts"].shape
    w = SWA_WINDOW
    r = tq // w
    nblk = S // w
    prev = lambda b, i: (b, jnp.maximum(i * r - 1, 0), 0)
    nxt = lambda b, i: (b, jnp.minimum((i + 1) * r, nblk - 1), 0)
    prev_t = lambda b, i: (b, 0, jnp.maximum(i * r - 1, 0))
    nxt_t = lambda b, i: (b, 0, jnp.minimum((i + 1) * r, nblk - 1))
    cur = lambda b, i: (b, i, 0)
    cur_t = lambda b, i: (b, 0, i)
    return pl.pallas_call(
        _swa_kernel,
        grid=(B, S // tq),
        in_specs=[pl.BlockSpec((1, 512, tq), cur_t),
                  pl.BlockSpec((1, tq, 128), cur), pl.BlockSpec((1, w, 128), prev), pl.BlockSpec((1, w, 128), nxt),
                  pl.BlockSpec((1, 128, tq), cur_t), pl.BlockSpec((1, 128, w), prev_t),
                  pl.BlockSpec((1, 128, w), nxt_t),
                  pl.BlockSpec((1, 1, tq), cur_t),
                  pl.BlockSpec((1, tq, 1), cur), pl.BlockSpec((1, w, 1), prev), pl.BlockSpec((1, w, 1), nxt),
                  pl.BlockSpec(sinks.shape, lambda b, i: (0, 0))],
        out_specs=pl.BlockSpec((1, tq, 512), cur),
        out_shape=jax.ShapeDtypeStruct((B, S, 512), BF16),
        compiler_params=_params("parallel", "parallel"),
        name="swa_attn",
    )(p["qts"], p["ks"], p["ks"], p["ks"], p["vts"], p["vts"], p["vts"],
      posq, posk, posk, posk, sinks)


def _memkv_kernel(mem_ref, g_ref, w_ref, kv_ref):
    mem_n = _rms_rows(mem_ref[0], g_ref[...]).astype(BF16)
    kv_ref[0] = _dot(mem_n, w_ref[...]).astype(BF16)


def _memkv_call(mem, g, w):
    B, M, _ = mem.shape
    return pl.pallas_call(
        _memkv_kernel,
        grid=(B,),
        in_specs=[pl.BlockSpec((1, M, D_MODEL), lambda b: (b, 0, 0)),
                  pl.BlockSpec(g.shape, lambda b: (0, 0)), pl.BlockSpec(w.shape, lambda b: (0, 0))],
        out_specs=pl.BlockSpec((1, M, 2 * D_MODEL), lambda b: (b, 0, 0)),
        out_shape=jax.ShapeDtypeStruct((B, M, 2 * D_MODEL), BF16),
        compiler_params=_params("parallel"),
        name="mem_kv",
    )(mem, g, w)


def _post_kernel(x_ref, od_ref, om_ref, os_ref, wout_ref, gmp_ref, gxp_ref, wxq_ref, kv_ref, wxo_ref,
                 gxo_ref, o_ref):
    mix = (_dot(od_ref[0], wout_ref[0:256, :]) + _dot(om_ref[0], wout_ref[256:512, :])
           + _dot(os_ref[0], wout_ref[512:1024, :]))
    x1 = x_ref[0] + _rms_rows(mix, gmp_ref[...])
    hq = _rms_rows(x1, gxp_ref[...]).astype(BF16)
    q = (_dot(hq, wxq_ref[...]) * (LOG2E / math.sqrt(X_HEAD_DIM))).astype(BF16)
    heads = []
    for hd in range(X_HEADS):
        c0 = hd * X_HEAD_DIM
        s = _nt_dot(q[:, c0:c0 + X_HEAD_DIM], kv_ref[0, :, c0:c0 + X_HEAD_DIM])
        e = jnp.exp2(s - jnp.max(s, axis=-1, keepdims=True))
        den = jnp.sum(e, axis=-1, keepdims=True)
        o = _dot(e.astype(BF16), kv_ref[0, :, D_MODEL + c0:D_MODEL + c0 + X_HEAD_DIM])
        heads.append((o / den).astype(BF16))
    xo = _dot(jnp.concatenate(heads, axis=-1), wxo_ref[...])
    o_ref[0] = x1 + _rms_rows(xo, gxo_ref[...])


def _post_call(x, od, om, os_, wout, gmp, gxp, wxq, kv, wxo, gxo, ts):
    B, S, _ = x.shape
    row = lambda n: pl.BlockSpec((1, ts, n), lambda b, i: (b, i, 0))
    full = lambda a: pl.BlockSpec(a.shape, lambda b, i: (0,) * a.ndim)
    kv_spec = pl.BlockSpec((1,) + kv.shape[1:], lambda b, i: (b, 0, 0))
    return pl.pallas_call(
        _post_kernel,
        grid=(B, S // ts),
        in_specs=[row(D_MODEL), row(256), row(256), row(512), full(wout), full(gmp), full(gxp), full(wxq),
                  kv_spec, full(wxo), full(gxo)],
        out_specs=row(D_MODEL),
        out_shape=jax.ShapeDtypeStruct(x.shape, F32),
        compiler_params=_params("parallel", "parallel"),
        name="mix_cross",
    )(x, od, om, os_, wout, gmp, gxp, wxq, kv, wxo, gxo)


def _ffn_kernel(x_ref, gpre_ref, win_ref, wout_ref, gpost_ref, o_ref):
    x = x_ref[0]
    h = _rms_rows(x, gpre_ref[...]).astype(BF16)
    gate = _dot(h, win_ref[:, 0:D_FF])
    up = _dot(h, win_ref[:, D_FF:2 * D_FF])
    f = _dot((gate * jax.nn.sigmoid(gate) * up).astype(BF16), wout_ref[...])
    o_ref[0] = x + _rms_rows(f, gpost_ref[...])


def _ffn_call(x, gpre, win, wout, gpost, ts):
    B, S, _ = x.shape
    row = pl.BlockSpec((1, ts, D_MODEL), lambda b, i: (b, i, 0))
    full = lambda a: pl.BlockSpec(a.shape, lambda b, i: (0,) * a.ndim)
    return pl.pallas_call(
        _ffn_kernel,
        grid=(B, S // ts),
        in_specs=[row, full(gpre), full(win), full(wout), full(gpost)],
        out_specs=row,
        out_shape=jax.ShapeDtypeStruct(x.shape, F32),
        compiler_params=_params("parallel", "parallel"),
        name="ffn",
    )(x, gpre, win, wout, gpost)


def _prep_weights(w_in, mla_q_norm_g, mla_w_q_up, mla_kv_norm_g, mla_w_kv_up):
    swap = np.concatenate([np.arange(16, 32), np.arange(0, 16)])
    kr = w_in[:, O_BKR:O_CQ]
    zeros64 = jnp.zeros((D_MODEL, 64), w_in.dtype)
    ak = w_in[:, O_AK:O_AV]
    ak_slabs = [a for hd in range(DIFF_HEADS) for a in (ak[:, 64 * hd:64 * (hd + 1)], zeros64)]
    wnat = jnp.concatenate(
        ak_slabs + [w_in[:, O_BCQ:O_BCKV], w_in[:, O_BCKV:O_BKR], kr, kr[:, swap], zeros64,
                    w_in[:, O_CK:O_CV]], axis=1)
    wtr = jnp.concatenate(
        [w_in[:, O_AQ:O_AK], w_in[:, O_AV:O_BCQ], w_in[:, O_CQ:O_CK], w_in[:, O_CV:O_END]], axis=1).T
    qup = mla_w_q_up.reshape(MLA_Q_RANK, MLA_HEADS, MLA_NOPE + MLA_ROPE)
    wqa = jnp.concatenate([qup, jnp.zeros((MLA_Q_RANK, MLA_HEADS, 32), qup.dtype)], axis=2)
    wqa = wqa.reshape(MLA_Q_RANK, MLA_HEADS * LANE).T
    wqb = qup[:, :, MLA_NOPE:][:, :, swap].reshape(MLA_Q_RANK, MLA_HEADS * MLA_ROPE).T
    kvup = mla_w_kv_up.reshape(MLA_KV_RANK, MLA_HEADS, MLA_NOPE + MLA_V)
    wk = jnp.concatenate([kvup[:, :, :MLA_NOPE], jnp.zeros((MLA_KV_RANK, MLA_HEADS, 64), kvup.dtype)], axis=2)
    wk = wk.reshape(MLA_KV_RANK, MLA_HEADS * LANE)
    wvt = kvup[:, :, MLA_NOPE:].reshape(MLA_KV_RANK, MLA_HEADS * MLA_V).T
    place = np.zeros((LANE, MLA_HEADS * LANE), np.float32)
    group_d = np.zeros((DIFF_HEADS * LANE, LANE), np.float32)
    group_m = np.zeros((MLA_HEADS * LANE, LANE), np.float32)
    for hd in range(MLA_HEADS):
        place[np.arange(32), hd * LANE + MLA_NOPE + np.arange(32)] = 1.0
        group_m[hd * LANE + np.arange(MLA_NOPE + MLA_ROPE), DIFF_MAPS + hd] = 1.0
    for mp in range(DIFF_MAPS):
        group_d[(mp // 2) * LANE + (mp % 2) * DIFF_QK + np.arange(DIFF_QK), mp] = 1.0
    return {
        "group_d": jnp.asarray(group_d, BF16), "group_m": jnp.asarray(group_m, BF16),
        "wnat": wnat.astype(BF16), "wtr": wtr.astype(BF16),
        "gq": mla_q_norm_g.reshape(1, -1), "gkv": mla_kv_norm_g.reshape(1, -1),
        "wqa": wqa.astype(BF16), "wqb": wqb.astype(BF16), "wk": wk.astype(BF16), "wvt": wvt.astype(BF16),
        "place": jnp.asarray(place, BF16),
    }


def _rope_tables(positions):
    half = MLA_ROPE // 2
    inv = ROPE_THETA ** (-jnp.arange(half, dtype=F32) / half)
    ang = positions.astype(F32)[..., None] * inv
    cos, sin = jnp.cos(ang), jnp.sin(ang)
    cos2 = jnp.concatenate([cos, cos], axis=-1)
    sin2 = jnp.concatenate([-sin, sin], axis=-1)
    tab = jnp.concatenate([cos2, sin2, jnp.zeros(cos2.shape[:2] + (64,), F32)], axis=-1)
    return {"tab": tab, "cos_t": cos2.swapaxes(1, 2), "sin_t": sin2.swapaxes(1, 2)}


def _fast_tiles(qstat, kstat, nq, nk):
    B, nt = qstat.shape[:2]
    assert nt % nq == 0 and nt % nk == 0, (nt, nq, nk)
    qn = qstat[:, :, :N_MAPS, 0].reshape(B, nq, -1, N_MAPS).max(axis=2)
    kn = kstat[:, :, 0, :N_MAPS].reshape(B, nk, -1, N_MAPS).max(axis=2)
    bound2 = qn[:, :, None, :] * kn[:, None, :, :]
    margin = 0.97
    return jnp.all(bound2 <= (margin * LOGIT_LIMIT) ** 2, axis=-1).astype(jnp.int32)


def _tile(n, want):
    t = min(n, want)
    assert n % t == 0, (n, t)
    return t


def kernel(x, mem, positions, g_mix_pre, g_mix_post, w_in, diff_lambda, diff_head_g, mla_q_norm_g, mla_w_q_up,
           mla_kv_norm_g, mla_w_kv_up, swa_sinks, w_out, g_x_pre, g_x_mem, g_x_post, w_xq, w_xkv, w_xo,
           g_ffn_pre, g_ffn_post, w_ffn_in, w_ffn_out):
    B, S, _ = x.shape
    depth = w_in.shape[0]
    ts_proj, ts_post, ts_ffn = _tile(S, 512), _tile(S, 512), _tile(S, 256)
    tq, tk, tq_swa = _tile(S, 2048), _tile(S, 512), _tile(S, 512)

    tabs = _rope_tables(positions)
    posq = positions.reshape(B, 1, S)
    posk = positions.reshape(B, S, 1)
    row = lambda v: v.reshape(1, -1)

    for l in range(depth):
        w = _prep_weights(w_in[l], mla_q_norm_g[l], mla_w_q_up[l], mla_kv_norm_g[l], mla_w_kv_up[l])
        names = ("qtd", "kd", "vtd", "qtm", "km", "vtm", "qts", "ks", "vts", "qstat", "kstat")
        p = dict(zip(names, _proj_call(x, row(g_mix_pre[l]), w, tabs, ts_proj)))
        linit = jnp.full((1, 1), 0.8 - 0.6 * math.exp(-0.3 * l), F32)
        fast = _fast_tiles(p["qstat"], p["kstat"], S // tq, S // tk)
        od, om = _dense_call(p, fast, posq, posk, diff_lambda[l], diff_head_g[l].reshape(-1, 1), linit, tq, tk)
        os_ = _swa_call(p, posq, posk, swa_sinks[l].reshape(-1, 1), tq_swa)
        kv = _memkv_call(mem, row(g_x_mem[l]), w_xkv[l].astype(BF16))
        x = _post_call(x, od, om, os_, w_out[l].astype(BF16), row(g_mix_post[l]), row(g_x_pre[l]),
                       w_xq[l].astype(BF16), kv, w_xo[l].astype(BF16), row(g_x_post[l]), ts_post)
        x = _ffn_call(x, row(g_ffn_pre[l]), w_ffn_in[l].astype(BF16), w_ffn_out[l].astype(BF16),
                      row(g_ffn_post[l]), ts_ffn)
    return x
```

```python
import math

import jax
import jax.numpy as jnp
import numpy as np
from jax import lax
from jax.experimental import pallas as pl
from jax.experimental.pallas import tpu as pltpu

F32 = jnp.float32
BF16 = jnp.bfloat16

D_MODEL = 1024
HEAD_DIM = 64
DIFF_HEADS = 4
DIFF_QK = 32
DIFF_MAPS = 2 * DIFF_HEADS
MLA_HEADS = 4
MLA_Q_RANK = 256
MLA_KV_RANK = 128
MLA_NOPE = 64
MLA_ROPE = 32
MLA_V = 64
SWA_Q_HEADS = 8
SWA_KV_HEADS = 2
SWA_GROUP = SWA_Q_HEADS // SWA_KV_HEADS
SWA_WINDOW = 128
X_HEADS = 4
X_HEAD_DIM = D_MODEL // X_HEADS
D_FF = 2816
N_ALIBI = DIFF_HEADS + SWA_Q_HEADS
ROPE_THETA = 10000.0
EPS = 1e-6
LOG2E = math.log2(math.e)
NEG_BIG = -1e30

LANE = 128
VMEM_LIMIT = 56 * 1024 * 1024

_OFF = np.cumsum([0, 256, 256, 256, MLA_Q_RANK, MLA_KV_RANK, MLA_ROPE, 512, 128, 128])
(O_AQ, O_AK, O_AV, O_BCQ, O_BCKV, O_BKR, O_CQ, O_CK, O_CV, O_END) = [int(v) for v in _OFF]

ONES_ROWS = 16
VT_ROWS = 64 + ONES_ROWS
AUG_LANES = 3
DIFF_AUG = 64
MLA_AUG = 96
LOGIT_LIMIT = 40.0
SPREAD_LIMIT = 20.0


def _alibi_slopes():
    return [2.0 ** (-8.0 * i / N_ALIBI) for i in range(1, N_ALIBI + 1)]


SWA_SLOPES = _alibi_slopes()[:SWA_Q_HEADS]
DIFF_SLOPES = _alibi_slopes()[SWA_Q_HEADS:]


def _nt_dot(a, b):
    return lax.dot_general(a, b, (((1,), (1,)), ((), ())), preferred_element_type=F32)


def _dot(a, b):
    return jnp.dot(a, b, preferred_element_type=F32)


def _rms_rows(x, g):
    return x * lax.rsqrt(jnp.mean(x * x, axis=-1, keepdims=True) + EPS) * g


def _params(*sem, flags=None):
    return pltpu.CompilerParams(dimension_semantics=sem, vmem_limit_bytes=VMEM_LIMIT, flags=flags)


def _store_values_t(vt_ref, vt, n_heads):
    ones = jnp.ones((ONES_ROWS, vt.shape[1]), BF16)
    for hd in range(n_heads):
        vt_ref[0, hd * VT_ROWS:hd * VT_ROWS + 64, :] = vt[hd * 64:(hd + 1) * 64].astype(BF16)
        vt_ref[0, hd * VT_ROWS + 64:(hd + 1) * VT_ROWS, :] = ones


def _ones_lanes(width, first):
    lane = lax.broadcasted_iota(jnp.int32, (1, width), 1) % LANE
    return jnp.where((lane >= first) & (lane < first + AUG_LANES), 1.0, 0.0)


def _proj_kernel(x_ref, g_ref, wnat_ref, wtr_ref, gq_ref, gkv_ref, wqa_ref, wqb_ref, wk_ref, wvt_ref,
                 pl_ref, gd_ref, gm_ref, tab_ref, cost_ref, sint_ref,
                 qtd_ref, kd_ref, vtd_ref, qtm_ref, km_ref, vtm_ref, qts_ref, ks_ref, vts_ref,
                 qst_ref, kst_ref):
    h = _rms_rows(x_ref[0], g_ref[...]).astype(BF16)
    nat = _dot(h, wnat_ref[...])
    tr = _nt_dot(wtr_ref[...], h)
    ts = nat.shape[0]

    c_diff = LOG2E / math.sqrt(DIFF_QK)
    c_mla = LOG2E / math.sqrt(MLA_NOPE + MLA_ROPE)
    c_swa = LOG2E / math.sqrt(HEAD_DIM)

    qd = tr[0:256] * c_diff
    qtd_ref[0] = qd.astype(BF16)
    _store_values_t(vtd_ref, tr[256:512], DIFF_HEADS)
    qts_ref[0] = (tr[512:1024] * c_swa).astype(BF16)
    vts_ref[0] = tr[1024:1152].astype(BF16)
    kd = nat[:, 0:512]
    kd_ref[0] = (kd + _ones_lanes(512, DIFF_AUG)).astype(BF16)
    ks_ref[0] = nat[:, 1024:1152].astype(BF16)

    cq_n = _rms_rows(nat[:, 512:768], gq_ref[...]).astype(BF16)
    ckv_n = _rms_rows(nat[:, 768:896], gkv_ref[...]).astype(BF16)

    u = nat[:, 896:1024] * tab_ref[0]
    kr = (u + pltpu.roll(u, 96, 1)).astype(BF16)
    km = _dot(ckv_n, wk_ref[...]) + _dot(kr, pl_ref[...])
    km_ref[0] = (km + _ones_lanes(512, MLA_AUG)).astype(BF16)
    _store_values_t(vtm_ref, _nt_dot(wvt_ref[...], ckv_n), MLA_HEADS)

    qa = _nt_dot(wqa_ref[...], cq_n)
    qb = _nt_dot(wqb_ref[...], cq_n)
    cos_t = cost_ref[0]
    sin_t = sint_ref[0]
    for hd in range(MLA_HEADS):
        r0 = hd * LANE
        qtm_ref[0, r0:r0 + 64, :] = (qa[r0:r0 + 64] * c_mla).astype(BF16)
        rope = qa[r0 + 64:r0 + 96] * cos_t + qb[hd * 32:hd * 32 + 32] * sin_t
        qtm_ref[0, r0 + 64:r0 + 96, :] = (rope * c_mla).astype(BF16)
        qtm_ref[0, r0 + 96:r0 + 128, :] = jnp.zeros((32, rope.shape[1]), BF16)

    qn = [jnp.sum((qd * qd).reshape(DIFF_MAPS, DIFF_QK, ts), axis=1)]
    qm = (qa * qa).reshape(MLA_HEADS, LANE, ts)
    qn.append(jnp.sum(qm, axis=1) * (c_mla * c_mla))
    qn.append(jnp.zeros((16 - DIFF_MAPS - MLA_HEADS, ts), F32))
    qmax = jnp.max(jnp.concatenate(qn, axis=0), axis=1, keepdims=True)
    qst_ref[0, 0] = jnp.broadcast_to(qmax, (16, LANE))
    kn = _dot((kd * kd).astype(BF16), gd_ref[...]) + _dot((km * km).astype(BF16), gm_ref[...])
    kst_ref[0, 0] = jnp.broadcast_to(jnp.max(kn, axis=0, keepdims=True), (8, LANE))


def _proj_call(x, g, w, tabs, ts):
    B, S, _ = x.shape
    row = lambda n: pl.BlockSpec((1, ts, n), lambda b, i: (b, i, 0))
    col = lambda n: pl.BlockSpec((1, n, ts), lambda b, i: (b, 0, i))
    full = lambda a: pl.BlockSpec(a.shape, lambda b, i: (0,) * a.ndim)
    consts = [g, w["wnat"], w["wtr"], w["gq"], w["gkv"], w["wqa"], w["wqb"], w["wk"], w["wvt"], w["place"],
              w["group_d"], w["group_m"]]
    n_vt = DIFF_HEADS * VT_ROWS
    nt = S // ts
    stat = lambda r: pl.BlockSpec((1, 1, r, LANE), lambda b, i: (b, i, 0, 0))
    out_shape = [
        jax.ShapeDtypeStruct((B, 256, S), BF16), jax.ShapeDtypeStruct((B, S, 512), BF16),
        jax.ShapeDtypeStruct((B, n_vt, S), BF16), jax.ShapeDtypeStruct((B, 512, S), BF16),
        jax.ShapeDtypeStruct((B, S, 512), BF16), jax.ShapeDtypeStruct((B, n_vt, S), BF16),
        jax.ShapeDtypeStruct((B, 512, S), BF16), jax.ShapeDtypeStruct((B, S, 128), BF16),
        jax.ShapeDtypeStruct((B, 128, S), BF16),
        jax.ShapeDtypeStruct((B, nt, 16, LANE), F32), jax.ShapeDtypeStruct((B, nt, 8, LANE), F32),
    ]
    out_specs = [col(256), row(512), col(n_vt), col(512), row(512), col(n_vt), col(512), row(128), col(128),
                 stat(16), stat(8)]
    return pl.pallas_call(
        _proj_kernel,
        grid=(B, S // ts),
        in_specs=[row(D_MODEL)] + [full(a) for a in consts] + [row(128), col(32), col(32)],
        out_specs=out_specs,
        out_shape=out_shape,
        compiler_params=_params("parallel", "parallel"),
        name="in_proj",
    )(x, *consts, tabs["tab"], tabs["cos_t"], tabs["sin_t"])


STRIP = 256
SLOW_DEPTH = 3
FAST_DEPTH = 2
FAST_KEYS = 512
N_MAPS = DIFF_MAPS + MLA_HEADS


def _split3(x):
    hi = x.astype(BF16).astype(F32)
    mid = (x - hi).astype(BF16).astype(F32)
    lo = (x - hi - mid).astype(BF16).astype(F32)
    return hi, mid, lo


def _dense_kernel(fast_ref, qtd_ref, kd_ref, vtd_ref, qtm_ref, km_ref, vtm_ref, posq_ref, posk_ref,
                  lam_ref, hg_ref, linit_ref, od_ref, om_ref,
                  wq_ref, m_ref, alpha_ref, acc_ref, s_ref):
    b, qi, ki = pl.program_id(0), pl.program_id(1), pl.program_id(2)
    tq = qtd_ref.shape[2]
    n_slots = s_ref.shape[0]

    @pl.when(ki == 0)
    def _init():
        m_ref[...] = jnp.full(m_ref.shape, NEG_BIG, F32)
        acc_ref[...] = jnp.zeros(acc_ref.shape, F32)
        for mp in range(DIFF_MAPS):
            q = qtd_ref[0, mp * DIFF_QK:(mp + 1) * DIFF_QK, :]
            above = jnp.zeros(((mp % 2) * DIFF_QK, tq), BF16)
            below = jnp.zeros((LANE - (mp % 2 + 1) * DIFF_QK, tq), BF16)
            wq_ref[mp] = jnp.concatenate([above, q, below] if mp % 2 else [q, below], axis=0)
        for hd in range(MLA_HEADS):
            wq_ref[DIFF_MAPS + hd] = qtm_ref[0, hd * LANE:(hd + 1) * LANE, :]

    def key_slab(mp, keys):
        if mp < DIFF_MAPS:
            return kd_ref[0, keys, (mp // 2) * LANE:(mp // 2 + 1) * LANE]
        hd = mp - DIFF_MAPS
        return km_ref[0, keys, hd * LANE:(hd + 1) * LANE]

    def values_t(mp, keys):
        if mp < DIFF_MAPS:
            return vtd_ref[0, (mp // 2) * VT_ROWS:(mp // 2 + 1) * VT_ROWS, keys]
        hd = mp - DIFF_MAPS
        return vtm_ref[0, hd * VT_ROWS:(hd + 1) * VT_ROWS, keys]

    def set_shift_rows(shift):
        rows = lax.broadcasted_iota(jnp.int32, (ONES_ROWS, tq), 0)
        for mp in range(N_MAPS):
            hi, mid, lo = _split3(shift[mp:mp + 1, :])
            block = jnp.where(rows == 0, hi, jnp.where(rows == 1, mid, jnp.where(rows == 2, lo, 0.0)))
            first = DIFF_AUG if mp < DIFF_MAPS else MLA_AUG
            wq_ref[mp, first:first + ONES_ROWS, :] = block.astype(BF16)

    def work_items(chunk):
        items = []
        for st in range(tq // STRIP):
            lanes = slice(st * STRIP, (st + 1) * STRIP)
            for kc in range(posk_ref.shape[1] // chunk):
                keys = slice(kc * chunk, (kc + 1) * chunk)
                dist = jnp.abs(posk_ref[0, keys] - posq_ref[0, :, lanes]).astype(F32)
                bias = [dist * (slope * LOG2E) for slope in DIFF_SLOPES]
                items += [(mp, lanes, keys, bias) for mp in range(N_MAPS)]
        return items

    def logits(mp, lanes, keys, bias):
        s = _dot(key_slab(mp, keys), wq_ref[mp, :, lanes])
        return s - bias[mp // 2] if mp < DIFF_MAPS else s

    fast = fast_ref[b, qi, ki] != 0

    @pl.when(fast)
    def _fast_step():
        m_old = m_ref[...]
        kmin = jnp.min(posk_ref[0], axis=0, keepdims=True)
        kmax = jnp.max(posk_ref[0], axis=0, keepdims=True)
        posq = posq_ref[0]
        outside = jnp.maximum(jnp.maximum(kmin - posq, posq - kmax), 0)
        dmin = (outside + (kmax - kmin)).astype(F32)
        rows = lax.broadcasted_iota(jnp.int32, m_old.shape, 0)
        slope = jnp.zeros(m_old.shape, F32)
        for hd in range(DIFF_HEADS):
            slope = jnp.where(rows // 2 == hd, DIFF_SLOPES[hd] * LOG2E, slope)
        m_eff = jnp.maximum(m_old, -LOGIT_LIMIT - slope * dmin)
        alpha_ref[...] = jnp.exp2(m_old - m_eff)
        m_ref[...] = m_eff
        set_shift_rows(-m_eff)
        items = work_items(FAST_KEYS)
        vals = {}
        for t in range(len(items) + FAST_DEPTH):
            if t < len(items):
                vals[t] = logits(*items[t])
            if t >= FAST_DEPTH:
                mp, lanes, keys, _ = items[t - FAST_DEPTH]
                p = jnp.exp2(vals.pop(t - FAST_DEPTH)).astype(BF16)
                old = acc_ref[mp, :, lanes]
                if keys.start == 0:
                    old = alpha_ref[mp:mp + 1, lanes] * old
                acc_ref[mp, :, lanes] = old + _dot(values_t(mp, keys), p)

    @pl.when(jnp.logical_not(fast))
    def _slow_step():
        set_shift_rows(jnp.zeros(m_ref.shape, F32))
        items = work_items(posk_ref.shape[1])

        def stage_qk(j):
            s_ref[j % n_slots] = logits(*items[j])

        def stage_max(j):
            mp, lanes, _, _ = items[j]
            m_old = m_ref[mp:mp + 1, lanes]
            m_new = jnp.maximum(m_old, jnp.max(s_ref[j % n_slots], axis=0, keepdims=True))
            m_ref[mp:mp + 1, lanes] = m_new
            return m_new, jnp.exp2(m_old - m_new)

        def stage_pv(j, m_new, alpha):
            mp, lanes, keys, _ = items[j]
            p = jnp.exp2((s_ref[j % n_slots] - m_new).astype(BF16))
            acc_ref[mp, :, lanes] = alpha * acc_ref[mp, :, lanes] + _dot(values_t(mp, keys), p)

        stats = {}
        for t in range(len(items) + SLOW_DEPTH):
            if t < len(items):
                stage_qk(t)
            if 1 <= t <= len(items):
                stats[t - 1] = stage_max(t - 1)
            if t >= SLOW_DEPTH:
                stage_pv(t - SLOW_DEPTH, *stats.pop(t - SLOW_DEPTH))

    @pl.when(ki == pl.num_programs(2) - 1)
    def _finish():
        lp = lam_ref[...]
        lam_init = linit_ref[...]
        lam = (jnp.exp(jnp.sum(lp[0:1] * lp[1:2], axis=-1, keepdims=True))
               - jnp.exp(jnp.sum(lp[2:3] * lp[3:4], axis=-1, keepdims=True)) + lam_init)
        def normalised(idx):
            return acc_ref[idx, 0:64, :] / acc_ref[idx, 64:65, :]

        outs = []
        for hd in range(DIFF_HEADS):
            o = normalised(2 * hd) - lam * normalised(2 * hd + 1)
            o = o * lax.rsqrt(jnp.mean(o * o, axis=0, keepdims=True) + EPS) * hg_ref[...]
            outs.append(o * (1.0 - lam_init))
        od_ref[0] = jnp.concatenate(outs, axis=0).T.astype(BF16)
        outs = [normalised(DIFF_MAPS + hd) for hd in range(MLA_HEADS)]
        om_ref[0] = jnp.concatenate(outs, axis=0).T.astype(BF16)


def _dense_call(p, fast, posq, posk, lam, hg, linit, tq, tk):
    B, _, S = p["qtd"].shape
    qcol = lambda n: pl.BlockSpec((1, n, tq), lambda b, i, k, f: (b, 0, i))
    krow = lambda n: pl.BlockSpec((1, tk, n), lambda b, i, k, f: (b, k, 0))
    kcol = lambda n: pl.BlockSpec((1, n, tk), lambda b, i, k, f: (b, 0, k))
    full = lambda a: pl.BlockSpec(a.shape, lambda b, i, k, f: (0,) * a.ndim)
    n_vt = DIFF_HEADS * VT_ROWS
    grid_spec = pltpu.PrefetchScalarGridSpec(
        num_scalar_prefetch=1,
        grid=(B, S // tq, S // tk),
        in_specs=[qcol(256), krow(512), kcol(n_vt), qcol(512), krow(512), kcol(n_vt),
                  qcol(1), krow(1), full(lam), full(hg), full(linit)],
        out_specs=[pl.BlockSpec((1, tq, 256), lambda b, i, k, f: (b, i, 0))] * 2,
        scratch_shapes=[pltpu.VMEM((N_MAPS, LANE, tq), BF16),
                        pltpu.VMEM((16, tq), F32), pltpu.VMEM((16, tq), F32),
                        pltpu.VMEM((N_MAPS, VT_ROWS, tq), F32),
                        pltpu.VMEM((SLOW_DEPTH + 2, tk, STRIP), F32)],
    )
    return pl.pallas_call(
        _dense_kernel,
        grid_spec=grid_spec,
        out_shape=[jax.ShapeDtypeStruct((B, S, 256), BF16)] * 2,
        compiler_params=_params("parallel", "parallel", "arbitrary"),
        name="dense_attn",
    )(fast, p["qtd"], p["kd"], p["vtd"], p["qtm"], p["km"], p["vtm"], posq, posk, lam, hg, linit)


SWA_DEPTH = 5


def _swa_kernel(qt_ref, kc_ref, kp_ref, kn_ref, vc_ref, vp_ref, vn_ref,
                posq_ref, pkc_ref, pkp_ref, pkn_ref, sink_ref, o_ref):
    i = pl.program_id(1)
    last = pl.num_programs(1) - 1
    tq = qt_ref.shape[2]
    w = SWA_WINDOW
    n_ext = tq + 2 * w
    k_ext = jnp.concatenate([kp_ref[0], kc_ref[0], kn_ref[0]], axis=0)
    vt_ext = jnp.concatenate([vp_ref[0], vc_ref[0], vn_ref[0]], axis=1)
    pos_ext = jnp.concatenate([pkp_ref[0], pkc_ref[0], pkn_ref[0]], axis=0)
    row = lax.broadcasted_iota(jnp.int32, (n_ext, 1), 0)
    far = 4 * n_ext
    kidx = (row - w + jnp.where(row < w, jnp.where(i == 0, far, 0), 0)
            + jnp.where(row >= tq + w, jnp.where(i == last, far, 0), 0))
    zeros = jnp.zeros((HEAD_DIM, tq), BF16)
    wqs = []
    for hq in range(SWA_Q_HEADS):
        q_h = qt_ref[0, hq * HEAD_DIM:(hq + 1) * HEAD_DIM, :]
        wqs.append(jnp.concatenate([q_h, zeros] if hq // SWA_GROUP == 0 else [zeros, q_h], axis=0))

    def item_logits(r, hq, dist, ok):
        sink = sink_ref[hq:hq + 1, :] * LOG2E
        s = _dot(k_ext[r * w:(r + 3) * w], wqs[hq][:, r * w:(r + 1) * w]) - dist * (SWA_SLOPES[hq] * LOG2E)
        s = jnp.where(ok, s, NEG_BIG)
        m = jnp.maximum(sink, jnp.max(s, axis=0, keepdims=True))
        return s, m, sink

    def item_output(r, hq, s, m, sink):
        kv = hq // SWA_GROUP
        e = jnp.exp2(s - m)
        den = jnp.exp2(sink - m) + jnp.sum(e, axis=0, keepdims=True)
        return _dot(vt_ext[kv * HEAD_DIM:(kv + 1) * HEAD_DIM, r * w:(r + 3) * w], e.astype(BF16)) / den

    items = []
    for r in range(tq // w):
        dist = jnp.abs(pos_ext[r * w:(r + 3) * w] - posq_ref[0, :, r * w:(r + 1) * w]).astype(F32)
        qidx = lax.broadcasted_iota(jnp.int32, (1, w), 1) + r * w
        ok = jnp.abs(kidx[r * w:(r + 3) * w] - qidx) <= w
        items += [(r, hq, dist, ok) for hq in range(SWA_Q_HEADS)]
    outs = {}
    ahead = {}
    for t in range(len(items) + SWA_DEPTH):
        if t < len(items):
            ahead[t] = item_logits(*items[t])
        if t >= SWA_DEPTH:
            r, hq, _, _ = items[t - SWA_DEPTH]
            outs[(hq, r)] = item_output(r, hq, *ahead.pop(t - SWA_DEPTH))
    heads = [jnp.concatenate([outs[(hq, r)] for r in range(tq // w)], axis=1) for hq in range(SWA_Q_HEADS)]
    o_ref[0] = jnp.concatenate(heads, axis=0).T.astype(BF16)


def _swa_call(p, posq, posk, sinks, tq):
    B, _, S = p["qts"].shape
    w = SWA_WINDOW
    r = tq // w
    nblk = S // w
    prev = lambda b, i: (b, jnp.maximum(i * r - 1, 0), 0)
    nxt = lambda b, i: (b, jnp.minimum((i + 1) * r, nblk - 1), 0)
    prev_t = lambda b, i: (b, 0, jnp.maximum(i * r - 1, 0))
    nxt_t = lambda b, i: (b, 0, jnp.minimum((i + 1) * r, nblk - 1))
    cur = lambda b, i: (b, i, 0)
    cur_t = lambda b, i: (b, 0, i)
    return pl.pallas_call(
        _swa_kernel,
        grid=(B, S // tq),
        in_specs=[pl.BlockSpec((1, 512, tq), cur_t),
                  pl.BlockSpec((1, tq, 128), cur), pl.BlockSpec((1, w, 128), prev), pl.BlockSpec((1, w, 128), nxt),
                  pl.BlockSpec((1, 128, tq), cur_t), pl.BlockSpec((1, 128, w), prev_t),
                  pl.BlockSpec((1, 128, w), nxt_t),
                  pl.BlockSpec((1, 1, tq), cur_t),
                  pl.BlockSpec((1, tq, 1), cur), pl.BlockSpec((1, w, 1), prev), pl.BlockSpec((1, w, 1), nxt),
                  pl.BlockSpec(sinks.shape, lambda b, i: (0, 0))],
        out_specs=pl.BlockSpec((1, tq, 512), cur),
        out_shape=jax.ShapeDtypeStruct((B, S, 512), BF16),
        compiler_params=_params("parallel", "parallel"),
        name="swa_attn",
    )(p["qts"], p["ks"], p["ks"], p["ks"], p["vts"], p["vts"], p["vts"],
      posq, posk, posk, posk, sinks)


def _memkv_kernel(mem_ref, g_ref, w_ref, kv_ref):
    mem_n = _rms_rows(mem_ref[0], g_ref[...]).astype(BF16)
    kv_ref[0] = _dot(mem_n, w_ref[...]).astype(BF16)


def _memkv_call(mem, g, w):
    B, M, _ = mem.shape
    return pl.pallas_call(
        _memkv_kernel,
        grid=(B,),
        in_specs=[pl.BlockSpec((1, M, D_MODEL), lambda b: (b, 0, 0)),
                  pl.BlockSpec(g.shape, lambda b: (0, 0)), pl.BlockSpec(w.shape, lambda b: (0, 0))],
        out_specs=pl.BlockSpec((1, M, 2 * D_MODEL), lambda b: (b, 0, 0)),
        out_shape=jax.ShapeDtypeStruct((B, M, 2 * D_MODEL), BF16),
        compiler_params=_params("parallel"),
        name="mem_kv",
    )(mem, g, w)


def _post_kernel(x_ref, od_ref, om_ref, os_ref, wout_ref, gmp_ref, gxp_ref, wxq_ref, kv_ref, wxo_ref,
                 gxo_ref, o_ref):
    mix = (_dot(od_ref[0], wout_ref[0:256, :]) + _dot(om_ref[0], wout_ref[256:512, :])
           + _dot(os_ref[0], wout_ref[512:1024, :]))
    x1 = x_ref[0] + _rms_rows(mix, gmp_ref[...])
    hq = _rms_rows(x1, gxp_ref[...]).astype(BF16)
    q = (_dot(hq, wxq_ref[...]) * (LOG2E / math.sqrt(X_HEAD_DIM))).astype(BF16)
    heads = []
    for hd in range(X_HEADS):
        c0 = hd * X_HEAD_DIM
        s = _nt_dot(q[:, c0:c0 + X_HEAD_DIM], kv_ref[0, :, c0:c0 + X_HEAD_DIM])
        e = jnp.exp2(s - jnp.max(s, axis=-1, keepdims=True))
        den = jnp.sum(e, axis=-1, keepdims=True)
        o = _dot(e.astype(BF16), kv_ref[0, :, D_MODEL + c0:D_MODEL + c0 + X_HEAD_DIM])
        heads.append((o / den).astype(BF16))
    xo = _dot(jnp.concatenate(heads, axis=-1), wxo_ref[...])
    o_ref[0] = x1 + _rms_rows(xo, gxo_ref[...])


def _post_call(x, od, om, os_, wout, gmp, gxp, wxq, kv, wxo, gxo, ts):
    B, S, _ = x.shape
    row = lambda n: pl.BlockSpec((1, ts, n), lambda b, i: (b, i, 0))
    full = lambda a: pl.BlockSpec(a.shape, lambda b, i: (0,) * a.ndim)
    kv_spec = pl.BlockSpec((1,) + kv.shape[1:], lambda b, i: (b, 0, 0))
    return pl.pallas_call(
        _post_kernel,
        grid=(B, S // ts),
        in_specs=[row(D_MODEL), row(256), row(256), row(512), full(wout), full(gmp), full(gxp), full(wxq),
                  kv_spec, full(wxo), full(gxo)],
        out_specs=row(D_MODEL),
        out_shape=jax.ShapeDtypeStruct(x.shape, F32),
        compiler_params=_params("parallel", "parallel"),
        name="mix_cross",
    )(x, od, om, os_, wout, gmp, gxp, wxq, kv, wxo, gxo)


def _ffn_kernel(x_ref, gpre_ref, win_ref, wout_ref, gpost_ref, o_ref):
    x = x_ref[0]
    h = _rms_rows(x, gpre_ref[...]).astype(BF16)
    gate = _dot(h, win_ref[:, 0:D_FF])
    up = _dot(h, win_ref[:, D_FF:2 * D_FF])
    f = _dot((gate * jax.nn.sigmoid(gate) * up).astype(BF16), wout_ref[...])
    o_ref[0] = x + _rms_rows(f, gpost_ref[...])


def _ffn_call(x, gpre, win, wout, gpost, ts):
    B, S, _ = x.shape
    row = pl.BlockSpec((1, ts, D_MODEL), lambda b, i: (b, i, 0))
    full = lambda a: pl.BlockSpec(a.shape, lambda b, i: (0,) * a.ndim)
    return pl.pallas_call(
        _ffn_kernel,
        grid=(B, S // ts),
        in_specs=[row, full(gpre), full(win), full(wout), full(gpost)],
        out_specs=row,
        out_shape=jax.ShapeDtypeStruct(x.shape, F32),
        compiler_params=_params("parallel", "parallel"),
        name="ffn",
    )(x, gpre, win, wout, gpost)


def _prep_weights(w_in, mla_q_norm_g, mla_w_q_up, mla_kv_norm_g, mla_w_kv_up):
    swap = np.concatenate([np.arange(16, 32), np.arange(0, 16)])
    kr = w_in[:, O_BKR:O_CQ]
    zeros64 = jnp.zeros((D_MODEL, 64), w_in.dtype)
    ak = w_in[:, O_AK:O_AV]
    ak_slabs = [a for hd in range(DIFF_HEADS) for a in (ak[:, 64 * hd:64 * (hd + 1)], zeros64)]
    wnat = jnp.concatenate(
        ak_slabs + [w_in[:, O_BCQ:O_BCKV], w_in[:, O_BCKV:O_BKR], kr, kr[:, swap], zeros64,
                    w_in[:, O_CK:O_CV]], axis=1)
    wtr = jnp.concatenate(
        [w_in[:, O_AQ:O_AK], w_in[:, O_AV:O_BCQ], w_in[:, O_CQ:O_CK], w_in[:, O_CV:O_END]], axis=1).T
    qup = mla_w_q_up.reshape(MLA_Q_RANK, MLA_HEADS, MLA_NOPE + MLA_ROPE)
    wqa = jnp.concatenate([qup, jnp.zeros((MLA_Q_RANK, MLA_HEADS, 32), qup.dtype)], axis=2)
    wqa = wqa.reshape(MLA_Q_RANK, MLA_HEADS * LANE).T
    wqb = qup[:, :, MLA_NOPE:][:, :, swap].reshape(MLA_Q_RANK, MLA_HEADS * MLA_ROPE).T
    kvup = mla_w_kv_up.reshape(MLA_KV_RANK, MLA_HEADS, MLA_NOPE + MLA_V)
    wk = jnp.concatenate([kvup[:, :, :MLA_NOPE], jnp.zeros((MLA_KV_RANK, MLA_HEADS, 64), kvup.dtype)], axis=2)
    wk = wk.reshape(MLA_KV_RANK, MLA_HEADS * LANE)
    wvt = kvup[:, :, MLA_NOPE:].reshape(MLA_KV_RANK, MLA_HEADS * MLA_V).T
    place = np.zeros((LANE, MLA_HEADS * LANE), np.float32)
    group_d = np.zeros((DIFF_HEADS * LANE, LANE), np.float32)
    group_m = np.zeros((MLA_HEADS * LANE, LANE), np.float32)
    for hd in range(MLA_HEADS):
        place[np.arange(32), hd * LANE + MLA_NOPE + np.arange(32)] = 1.0
        group_m[hd * LANE + np.arange(MLA_NOPE + MLA_ROPE), DIFF_MAPS + hd] = 1.0
    for mp in range(DIFF_MAPS):
        group_d[(mp // 2) * LANE + (mp % 2) * DIFF_QK + np.arange(DIFF_QK), mp] = 1.0
    return {
        "group_d": jnp.asarray(group_d, BF16), "group_m": jnp.asarray(group_m, BF16),
        "wnat": wnat.astype(BF16), "wtr": wtr.astype(BF16),
        "gq": mla_q_norm_g.reshape(1, -1), "gkv": mla_kv_norm_g.reshape(1, -1),
        "wqa": wqa.astype(BF16), "wqb": wqb.astype(BF16), "wk": wk.astype(BF16), "wvt": wvt.astype(BF16),
        "place": jnp.asarray(place, BF16),
    }


def _rope_tables(positions):
    half = MLA_ROPE // 2
    inv = ROPE_THETA ** (-jnp.arange(half, dtype=F32) / half)
    ang = positions.astype(F32)[..., None] * inv
    cos, sin = jnp.cos(ang), jnp.sin(ang)
    cos2 = jnp.concatenate([cos, cos], axis=-1)
    sin2 = jnp.concatenate([-sin, sin], axis=-1)
    tab = jnp.concatenate([cos2, sin2, jnp.zeros(cos2.shape[:2] + (64,), F32)], axis=-1)
    return {"tab": tab, "cos_t": cos2.swapaxes(1, 2), "sin_t": sin2.swapaxes(1, 2)}


def _fast_tiles(qstat, kstat, positions, nq, nk):
    B, nt = qstat.shape[:2]
    assert nt % nq == 0 and nt % nk == 0, (nt, nq, nk)
    qn = qstat[:, :, :N_MAPS, 0].reshape(B, nq, -1, N_MAPS).max(axis=2)
    kn = kstat[:, :, 0, :N_MAPS].reshape(B, nk, -1, N_MAPS).max(axis=2)
    bound2 = qn[:, :, None, :] * kn[:, None, :, :]
    margin = 0.97
    bounded = jnp.all(bound2 <= (margin * LOGIT_LIMIT) ** 2, axis=-1)
    ktile = positions.reshape(B, nk, -1)
    spread = (ktile.max(axis=-1) - ktile.min(axis=-1)).astype(F32) * (max(DIFF_SLOPES) * LOG2E)
    return (bounded & (spread <= SPREAD_LIMIT)[:, None, :]).astype(jnp.int32)


def _tile(n, want):
    t = min(n, want)
    assert n % t == 0, (n, t)
    return t


def kernel(x, mem, positions, g_mix_pre, g_mix_post, w_in, diff_lambda, diff_head_g, mla_q_norm_g, mla_w_q_up,
           mla_kv_norm_g, mla_w_kv_up, swa_sinks, w_out, g_x_pre, g_x_mem, g_x_post, w_xq, w_xkv, w_xo,
           g_ffn_pre, g_ffn_post, w_ffn_in, w_ffn_out):
    B, S, _ = x.shape
    depth = w_in.shape[0]
    ts_proj, ts_post, ts_ffn = _tile(S, 512), _tile(S, 512), _tile(S, 256)
    tq, tk, tq_swa = _tile(S, 2048), _tile(S, 512), _tile(S, 512)

    tabs = _rope_tables(positions)
    posq = positions.reshape(B, 1, S)
    posk = positions.reshape(B, S, 1)
    row = lambda v: v.reshape(1, -1)

    for l in range(depth):
        w = _prep_weights(w_in[l], mla_q_norm_g[l], mla_w_q_up[l], mla_kv_norm_g[l], mla_w_kv_up[l])
        names = ("qtd", "kd", "vtd", "qtm", "km", "vtm", "qts", "ks", "vts", "qstat", "kstat")
        p = dict(zip(names, _proj_call(x, row(g_mix_pre[l]), w, tabs, ts_proj)))
        linit = jnp.full((1, 1), 0.8 - 0.6 * math.exp(-0.3 * l), F32)
        fast = _fast_tiles(p["qstat"], p["kstat"], positions, S // tq, S // tk)
        od, om = _dense_call(p, fast, posq, posk, diff_lambda[l], diff_head_g[l].reshape(-1, 1), linit, tq, tk)
        os_ = _swa_call(p, posq, posk, swa_sinks[l].reshape(-1, 1), tq_swa)
        kv = _memkv_call(mem, row(g_x_mem[l]), w_xkv[l].astype(BF16))
        x = _post_call(x, od, om, os_, w_out[l].astype(BF16), row(g_mix_post[l]), row(g_x_pre[l]),
                       w_xq[l].astype(BF16), kv, w_xo[l].astype(BF16), row(g_x_post[l]), ts_post)
        x = _ffn_call(x, row(g_ffn_pre[l]), w_ffn_in[l].astype(BF16), w_ffn_out[l].astype(BF16),
                      row(g_ffn_post[l]), ts_ffn)
    return x
```

```python
import math

import jax
import jax.numpy as jnp
import numpy as np
from jax import lax
from jax.experimental import pallas as pl
from jax.experimental.pallas import tpu as pltpu

F32 = jnp.float32
BF16 = jnp.bfloat16

D_MODEL = 1024
HEAD_DIM = 64
DIFF_HEADS = 4
DIFF_QK = 32
DIFF_MAPS = 2 * DIFF_HEADS
MLA_HEADS = 4
MLA_Q_RANK = 256
MLA_KV_RANK = 128
MLA_NOPE = 64
MLA_ROPE = 32
MLA_V = 64
SWA_Q_HEADS = 8
SWA_KV_HEADS = 2
SWA_GROUP = SWA_Q_HEADS // SWA_KV_HEADS
SWA_WINDOW = 128
X_HEADS = 4
X_HEAD_DIM = D_MODEL // X_HEADS
D_FF = 2816
N_ALIBI = DIFF_HEADS + SWA_Q_HEADS
ROPE_THETA = 10000.0
EPS = 1e-6
LOG2E = math.log2(math.e)
NEG_BIG = -1e30

LANE = 128
VMEM_LIMIT = 56 * 1024 * 1024

_OFF = np.cumsum([0, 256, 256, 256, MLA_Q_RANK, MLA_KV_RANK, MLA_ROPE, 512, 128, 128])
(O_AQ, O_AK, O_AV, O_BCQ, O_BCKV, O_BKR, O_CQ, O_CK, O_CV, O_END) = [int(v) for v in _OFF]

ONES_ROWS = 16
VT_ROWS = 64 + ONES_ROWS
AUG_LANES = 3
DIFF_AUG = 64
MLA_AUG = 96
LOGIT_LIMIT = 40.0
SPREAD_LIMIT = 20.0


def _alibi_slopes():
    return [2.0 ** (-8.0 * i / N_ALIBI) for i in range(1, N_ALIBI + 1)]


SWA_SLOPES = _alibi_slopes()[:SWA_Q_HEADS]
DIFF_SLOPES = _alibi_slopes()[SWA_Q_HEADS:]


def _nt_dot(a, b):
    return lax.dot_general(a, b, (((1,), (1,)), ((), ())), preferred_element_type=F32)


def _dot(a, b):
    return jnp.dot(a, b, preferred_element_type=F32)


def _rms_rows(x, g):
    return x * lax.rsqrt(jnp.mean(x * x, axis=-1, keepdims=True) + EPS) * g


def _params(*sem, flags=None):
    return pltpu.CompilerParams(dimension_semantics=sem, vmem_limit_bytes=VMEM_LIMIT, flags=flags)


def _store_values_t(vt_ref, vt, n_heads):
    ones = jnp.ones((ONES_ROWS, vt.shape[1]), BF16)
    for hd in range(n_heads):
        vt_ref[0, hd * VT_ROWS:hd * VT_ROWS + 64, :] = vt[hd * 64:(hd + 1) * 64].astype(BF16)
        vt_ref[0, hd * VT_ROWS + 64:(hd + 1) * VT_ROWS, :] = ones


def _ones_lanes(width, first):
    lane = lax.broadcasted_iota(jnp.int32, (1, width), 1) % LANE
    return jnp.where((lane >= first) & (lane < first + AUG_LANES), 1.0, 0.0)


def _proj_kernel(x_ref, g_ref, wnat_ref, wtr_ref, gq_ref, gkv_ref, wqa_ref, wqb_ref, wk_ref, wvt_ref,
                 pl_ref, gd_ref, gm_ref, tab_ref, cost_ref, sint_ref,
                 qtd_ref, kd_ref, vtd_ref, qtm_ref, km_ref, vtm_ref, qts_ref, ks_ref, vts_ref,
                 qst_ref, kst_ref):
    h = _rms_rows(x_ref[0], g_ref[...]).astype(BF16)
    nat = _dot(h, wnat_ref[...])
    tr = _nt_dot(wtr_ref[...], h)
    ts = nat.shape[0]

    c_diff = LOG2E / math.sqrt(DIFF_QK)
    c_mla = LOG2E / math.sqrt(MLA_NOPE + MLA_ROPE)
    c_swa = LOG2E / math.sqrt(HEAD_DIM)

    qd = tr[0:256] * c_diff
    qtd_ref[0] = qd.astype(BF16)
    _store_values_t(vtd_ref, tr[256:512], DIFF_HEADS)
    qts_ref[0] = (tr[512:1024] * c_swa).astype(BF16)
    vts_ref[0] = tr[1024:1152].astype(BF16)
    kd = nat[:, 0:512]
    kd_ref[0] = (kd + _ones_lanes(512, DIFF_AUG)).astype(BF16)
    ks_ref[0] = nat[:, 1024:1152].astype(BF16)

    cq_n = _rms_rows(nat[:, 512:768], gq_ref[...]).astype(BF16)
    ckv_n = _rms_rows(nat[:, 768:896], gkv_ref[...]).astype(BF16)

    u = nat[:, 896:1024] * tab_ref[0]
    kr = (u + pltpu.roll(u, 96, 1)).astype(BF16)
    km = _dot(ckv_n, wk_ref[...]) + _dot(kr, pl_ref[...])
    km_ref[0] = (km + _ones_lanes(512, MLA_AUG)).astype(BF16)
    _store_values_t(vtm_ref, _nt_dot(wvt_ref[...], ckv_n), MLA_HEADS)

    qa = _nt_dot(wqa_ref[...], cq_n)
    qb = _nt_dot(wqb_ref[...], cq_n)
    cos_t = cost_ref[0]
    sin_t = sint_ref[0]
    for hd in range(MLA_HEADS):
        r0 = hd * LANE
        qtm_ref[0, r0:r0 + 64, :] = (qa[r0:r0 + 64] * c_mla).astype(BF16)
        rope = qa[r0 + 64:r0 + 96] * cos_t + qb[hd * 32:hd * 32 + 32] * sin_t
        qtm_ref[0, r0 + 64:r0 + 96, :] = (rope * c_mla).astype(BF16)
        qtm_ref[0, r0 + 96:r0 + 128, :] = jnp.zeros((32, rope.shape[1]), BF16)

    qn = [jnp.sum((qd * qd).reshape(DIFF_MAPS, DIFF_QK, ts), axis=1)]
    qm = (qa * qa).reshape(MLA_HEADS, LANE, ts)
    qn.append(jnp.sum(qm, axis=1) * (c_mla * c_mla))
    qn.append(jnp.zeros((16 - DIFF_MAPS - MLA_HEADS, ts), F32))
    qmax = jnp.max(jnp.concatenate(qn, axis=0), axis=1, keepdims=True)
    qst_ref[0, 0] = jnp.broadcast_to(qmax, (16, LANE))
    kn = _dot((kd * kd).astype(BF16), gd_ref[...]) + _dot((km * km).astype(BF16), gm_ref[...])
    kst_ref[0, 0] = jnp.broadcast_to(jnp.max(kn, axis=0, keepdims=True), (8, LANE))


def _proj_call(x, g, w, tabs, ts):
    B, S, _ = x.shape
    row = lambda n: pl.BlockSpec((1, ts, n), lambda b, i: (b, i, 0))
    col = lambda n: pl.BlockSpec((1, n, ts), lambda b, i: (b, 0, i))
    full = lambda a: pl.BlockSpec(a.shape, lambda b, i: (0,) * a.ndim)
    consts = [g, w["wnat"], w["wtr"], w["gq"], w["gkv"], w["wqa"], w["wqb"], w["wk"], w["wvt"], w["place"],
              w["group_d"], w["group_m"]]
    n_vt = DIFF_HEADS * VT_ROWS
    nt = S // ts
    stat = lambda r: pl.BlockSpec((1, 1, r, LANE), lambda b, i: (b, i, 0, 0))
    out_shape = [
        jax.ShapeDtypeStruct((B, 256, S), BF16), jax.ShapeDtypeStruct((B, S, 512), BF16),
        jax.ShapeDtypeStruct((B, n_vt, S), BF16), jax.ShapeDtypeStruct((B, 512, S), BF16),
        jax.ShapeDtypeStruct((B, S, 512), BF16), jax.ShapeDtypeStruct((B, n_vt, S), BF16),
        jax.ShapeDtypeStruct((B, 512, S), BF16), jax.ShapeDtypeStruct((B, S, 128), BF16),
        jax.ShapeDtypeStruct((B, 128, S), BF16),
        jax.ShapeDtypeStruct((B, nt, 16, LANE), F32), jax.ShapeDtypeStruct((B, nt, 8, LANE), F32),
    ]
    out_specs = [col(256), row(512), col(n_vt), col(512), row(512), col(n_vt), col(512), row(128), col(128),
                 stat(16), stat(8)]
    return pl.pallas_call(
        _proj_kernel,
        grid=(B, S // ts),
        in_specs=[row(D_MODEL)] + [full(a) for a in consts] + [row(128), col(32), col(32)],
        out_specs=out_specs,
        out_shape=out_shape,
        compiler_params=_params("parallel", "parallel"),
        name="in_proj",
    )(x, *consts, tabs["tab"], tabs["cos_t"], tabs["sin_t"])


STRIP = 256
SLOW_DEPTH = 3
FAST_DEPTH = 2
FAST_KEYS = 512
N_MAPS = DIFF_MAPS + MLA_HEADS


def _split3(x):
    hi = x.astype(BF16).astype(F32)
    mid = (x - hi).astype(BF16).astype(F32)
    lo = (x - hi - mid).astype(BF16).astype(F32)
    return hi, mid, lo


def _dense_kernel(fast_ref, krange_ref, qtd_ref, kd_ref, vtd_ref, qtm_ref, km_ref, vtm_ref, posq_ref, posk_ref,
                  lam_ref, hg_ref, linit_ref, od_ref, om_ref,
                  wq_ref, m_ref, alpha_ref, acc_ref, s_ref):
    b, qi, ki = pl.program_id(0), pl.program_id(1), pl.program_id(2)
    tq = qtd_ref.shape[2]
    n_slots = s_ref.shape[0]

    @pl.when(ki == 0)
    def _init():
        m_ref[...] = jnp.full(m_ref.shape, NEG_BIG, F32)
        acc_ref[...] = jnp.zeros(acc_ref.shape, F32)
        for mp in range(DIFF_MAPS):
            q = qtd_ref[0, mp * DIFF_QK:(mp + 1) * DIFF_QK, :]
            above = jnp.zeros(((mp % 2) * DIFF_QK, tq), BF16)
            below = jnp.zeros((LANE - (mp % 2 + 1) * DIFF_QK, tq), BF16)
            wq_ref[mp] = jnp.concatenate([above, q, below] if mp % 2 else [q, below], axis=0)
        for hd in range(MLA_HEADS):
            wq_ref[DIFF_MAPS + hd] = qtm_ref[0, hd * LANE:(hd + 1) * LANE, :]

    def key_slab(mp, keys):
        if mp < DIFF_MAPS:
            return kd_ref[0, keys, (mp // 2) * LANE:(mp // 2 + 1) * LANE]
        hd = mp - DIFF_MAPS
        return km_ref[0, keys, hd * LANE:(hd + 1) * LANE]

    def values_t(mp, keys):
        if mp < DIFF_MAPS:
            return vtd_ref[0, (mp // 2) * VT_ROWS:(mp // 2 + 1) * VT_ROWS, keys]
        hd = mp - DIFF_MAPS
        return vtm_ref[0, hd * VT_ROWS:(hd + 1) * VT_ROWS, keys]

    def set_shift_rows(shift):
        rows = lax.broadcasted_iota(jnp.int32, (ONES_ROWS, tq), 0)
        for mp in range(N_MAPS):
            hi, mid, lo = _split3(shift[mp:mp + 1, :])
            block = jnp.where(rows == 0, hi, jnp.where(rows == 1, mid, jnp.where(rows == 2, lo, 0.0)))
            first = DIFF_AUG if mp < DIFF_MAPS else MLA_AUG
            wq_ref[mp, first:first + ONES_ROWS, :] = block.astype(BF16)

    def work_items(chunk):
        items = []
        for st in range(tq // STRIP):
            lanes = slice(st * STRIP, (st + 1) * STRIP)
            for kc in range(posk_ref.shape[1] // chunk):
                keys = slice(kc * chunk, (kc + 1) * chunk)
                dist = jnp.abs(posk_ref[0, keys] - posq_ref[0, :, lanes]).astype(F32)
                bias = [dist * (slope * LOG2E) for slope in DIFF_SLOPES]
                items += [(mp, lanes, keys, bias) for mp in range(N_MAPS)]
        return items

    def logits(mp, lanes, keys, bias):
        s = _dot(key_slab(mp, keys), wq_ref[mp, :, lanes])
        return s - bias[mp // 2] if mp < DIFF_MAPS else s

    fast = fast_ref[b, qi, ki] != 0

    @pl.when(fast)
    def _fast_step():
        m_old = m_ref[...]
        kmin, kmax = krange_ref[b, ki, 0], krange_ref[b, ki, 1]
        posq = posq_ref[0]
        outside = jnp.maximum(jnp.maximum(kmin - posq, posq - kmax), 0)
        dmin = (outside + (kmax - kmin)).astype(F32)
        rows = lax.broadcasted_iota(jnp.int32, m_old.shape, 0)
        slope = jnp.zeros(m_old.shape, F32)
        for hd in range(DIFF_HEADS):
            slope = jnp.where(rows // 2 == hd, DIFF_SLOPES[hd] * LOG2E, slope)
        m_eff = jnp.maximum(m_old, -LOGIT_LIMIT - slope * dmin)
        alpha_ref[...] = jnp.exp2(m_old - m_eff)
        m_ref[...] = m_eff
        set_shift_rows(-m_eff)
        items = work_items(FAST_KEYS)
        vals = {}
        for t in range(len(items) + FAST_DEPTH):
            if t < len(items):
                vals[t] = logits(*items[t])
            if t >= FAST_DEPTH:
                mp, lanes, keys, _ = items[t - FAST_DEPTH]
                p = jnp.exp2(vals.pop(t - FAST_DEPTH)).astype(BF16)
                old = acc_ref[mp, :, lanes]
                if keys.start == 0:
                    old = alpha_ref[mp:mp + 1, lanes] * old
                acc_ref[mp, :, lanes] = old + _dot(values_t(mp, keys), p)

    @pl.when(jnp.logical_not(fast))
    def _slow_step():
        set_shift_rows(jnp.zeros(m_ref.shape, F32))
        items = work_items(posk_ref.shape[1])

        def stage_qk(j):
            s_ref[j % n_slots] = logits(*items[j])

        def stage_max(j):
            mp, lanes, _, _ = items[j]
            m_old = m_ref[mp:mp + 1, lanes]
            m_new = jnp.maximum(m_old, jnp.max(s_ref[j % n_slots], axis=0, keepdims=True))
            m_ref[mp:mp + 1, lanes] = m_new
            return m_new, jnp.exp2(m_old - m_new)

        def stage_pv(j, m_new, alpha):
            mp, lanes, keys, _ = items[j]
            p = jnp.exp2((s_ref[j % n_slots] - m_new).astype(BF16))
            acc_ref[mp, :, lanes] = alpha * acc_ref[mp, :, lanes] + _dot(values_t(mp, keys), p)

        stats = {}
        for t in range(len(items) + SLOW_DEPTH):
            if t < len(items):
                stage_qk(t)
            if 1 <= t <= len(items):
                stats[t - 1] = stage_max(t - 1)
            if t >= SLOW_DEPTH:
                stage_pv(t - SLOW_DEPTH, *stats.pop(t - SLOW_DEPTH))

    @pl.when(ki == pl.num_programs(2) - 1)
    def _finish():
        lp = lam_ref[...]
        lam_init = linit_ref[...]
        lam = (jnp.exp(jnp.sum(lp[0:1] * lp[1:2], axis=-1, keepdims=True))
               - jnp.exp(jnp.sum(lp[2:3] * lp[3:4], axis=-1, keepdims=True)) + lam_init)
        def normalised(idx):
            return acc_ref[idx, 0:64, :] / acc_ref[idx, 64:65, :]

        outs = []
        for hd in range(DIFF_HEADS):
            o = normalised(2 * hd) - lam * normalised(2 * hd + 1)
            o = o * lax.rsqrt(jnp.mean(o * o, axis=0, keepdims=True) + EPS) * hg_ref[...]
            outs.append(o * (1.0 - lam_init))
        od_ref[0] = jnp.concatenate(outs, axis=0).T.astype(BF16)
        outs = [normalised(DIFF_MAPS + hd) for hd in range(MLA_HEADS)]
        om_ref[0] = jnp.concatenate(outs, axis=0).T.astype(BF16)


def _dense_call(p, fast, krange, posq, posk, lam, hg, linit, tq, tk):
    B, _, S = p["qtd"].shape
    qcol = lambda n: pl.BlockSpec((1, n, tq), lambda b, i, k, *_: (b, 0, i))
    krow = lambda n: pl.BlockSpec((1, tk, n), lambda b, i, k, *_: (b, k, 0))
    kcol = lambda n: pl.BlockSpec((1, n, tk), lambda b, i, k, *_: (b, 0, k))
    full = lambda a: pl.BlockSpec(a.shape, lambda b, i, k, *_: (0,) * a.ndim)
    n_vt = DIFF_HEADS * VT_ROWS
    grid_spec = pltpu.PrefetchScalarGridSpec(
        num_scalar_prefetch=2,
        grid=(B, S // tq, S // tk),
        in_specs=[qcol(256), krow(512), kcol(n_vt), qcol(512), krow(512), kcol(n_vt),
                  qcol(1), krow(1), full(lam), full(hg), full(linit)],
        out_specs=[pl.BlockSpec((1, tq, 256), lambda b, i, k, *_: (b, i, 0))] * 2,
        scratch_shapes=[pltpu.VMEM((N_MAPS, LANE, tq), BF16),
                        pltpu.VMEM((16, tq), F32), pltpu.VMEM((16, tq), F32),
                        pltpu.VMEM((N_MAPS, VT_ROWS, tq), F32),
                        pltpu.VMEM((SLOW_DEPTH + 2, tk, STRIP), F32)],
    )
    return pl.pallas_call(
        _dense_kernel,
        grid_spec=grid_spec,
        out_shape=[jax.ShapeDtypeStruct((B, S, 256), BF16)] * 2,
        compiler_params=_params("parallel", "parallel", "arbitrary"),
        name="dense_attn",
    )(fast, krange, p["qtd"], p["kd"], p["vtd"], p["qtm"], p["km"], p["vtm"], posq, posk, lam, hg, linit)


SWA_DEPTH = 5


def _swa_kernel(qt_ref, kc_ref, kp_ref, kn_ref, vc_ref, vp_ref, vn_ref,
                posq_ref, pkc_ref, pkp_ref, pkn_ref, sink_ref, o_ref):
    i = pl.program_id(1)
    last = pl.num_programs(1) - 1
    tq = qt_ref.shape[2]
    w = SWA_WINDOW
    n_ext = tq + 2 * w
    k_ext = jnp.concatenate([kp_ref[0], kc_ref[0], kn_ref[0]], axis=0)
    vt_ext = jnp.concatenate([vp_ref[0], vc_ref[0], vn_ref[0]], axis=1)
    pos_ext = jnp.concatenate([pkp_ref[0], pkc_ref[0], pkn_ref[0]], axis=0)
    row = lax.broadcasted_iota(jnp.int32, (n_ext, 1), 0)
    far = 4 * n_ext
    kidx = (row - w + jnp.where(row < w, jnp.where(i == 0, far, 0), 0)
            + jnp.where(row >= tq + w, jnp.where(i == last, far, 0), 0))
    zeros = jnp.zeros((HEAD_DIM, tq), BF16)
    wqs = []
    for hq in range(SWA_Q_HEADS):
        q_h = qt_ref[0, hq * HEAD_DIM:(hq + 1) * HEAD_DIM, :]
        wqs.append(jnp.concatenate([q_h, zeros] if hq // SWA_GROUP == 0 else [zeros, q_h], axis=0))

    def item_logits(r, hq, dist, ok):
        sink = sink_ref[hq:hq + 1, :] * LOG2E
        s = _dot(k_ext[r * w:(r + 3) * w], wqs[hq][:, r * w:(r + 1) * w]) - dist * (SWA_SLOPES[hq] * LOG2E)
        s = jnp.where(ok, s, NEG_BIG)
        m = jnp.maximum(sink, jnp.max(s, axis=0, keepdims=True))
        return s, m, sink

    def item_output(r, hq, s, m, sink):
        kv = hq // SWA_GROUP
        e = jnp.exp2(s - m)
        den = jnp.exp2(sink - m) + jnp.sum(e, axis=0, keepdims=True)
        return _dot(vt_ext[kv * HEAD_DIM:(kv + 1) * HEAD_DIM, r * w:(r + 3) * w], e.astype(BF16)) / den

    items = []
    for r in range(tq // w):
        dist = jnp.abs(pos_ext[r * w:(r + 3) * w] - posq_ref[0, :, r * w:(r + 1) * w]).astype(F32)
        qidx = lax.broadcasted_iota(jnp.int32, (1, w), 1) + r * w
        ok = jnp.abs(kidx[r * w:(r + 3) * w] - qidx) <= w
        items += [(r, hq, dist, ok) for hq in range(SWA_Q_HEADS)]
    outs = {}
    ahead = {}
    for t in range(len(items) + SWA_DEPTH):
        if t < len(items):
            ahead[t] = item_logits(*items[t])
        if t >= SWA_DEPTH:
            r, hq, _, _ = items[t - SWA_DEPTH]
            outs[(hq, r)] = item_output(r, hq, *ahead.pop(t - SWA_DEPTH))
    heads = [jnp.concatenate([outs[(hq, r)] for r in range(tq // w)], axis=1) for hq in range(SWA_Q_HEADS)]
    o_ref[0] = jnp.concatenate(heads, axis=0).T.astype(BF16)


def _swa_call(p, posq, posk, sinks, tq):
    B, _, S = p["qts"].shape
    w = SWA_WINDOW
    r = tq // w
    nblk = S // w
    prev = lambda b, i: (b, jnp.maximum(i * r - 1, 0), 0)
    nxt = lambda b, i: (b, jnp.minimum((i + 1) * r, nblk - 1), 0)
    prev_t = lambda b, i: (b, 0, jnp.maximum(i * r - 1, 0))
    nxt_t = lambda b, i: (b, 0, jnp.minimum((i + 1) * r, nblk - 1))
    cur = lambda b, i: (b, i, 0)
    cur_t = lambda b, i: (b, 0, i)
    return pl.pallas_call(
        _swa_kernel,
        grid=(B, S // tq),
        in_specs=[pl.BlockSpec((1, 512, tq), cur_t),
                  pl.BlockSpec((1, tq, 128), cur), pl.BlockSpec((1, w, 128), prev), pl.BlockSpec((1, w, 128), nxt),
                  pl.BlockSpec((1, 128, tq), cur_t), pl.BlockSpec((1, 128, w), prev_t),
                  pl.BlockSpec((1, 128, w), nxt_t),
                  pl.BlockSpec((1, 1, tq), cur_t),
                  pl.BlockSpec((1, tq, 1), cur), pl.BlockSpec((1, w, 1), prev), pl.BlockSpec((1, w, 1), nxt),
                  pl.BlockSpec(sinks.shape, lambda b, i: (0, 0))],
        out_specs=pl.BlockSpec((1, tq, 512), cur),
        out_shape=jax.ShapeDtypeStruct((B, S, 512), BF16),
        compiler_params=_params("parallel", "parallel"),
        name="swa_attn",
    )(p["qts"], p["ks"], p["ks"], p["ks"], p["vts"], p["vts"], p["vts"],
      posq, posk, posk, posk, sinks)


def _memkv_kernel(mem_ref, g_ref, w_ref, kv_ref):
    mem_n = _rms_rows(mem_ref[0], g_ref[...]).astype(BF16)
    kv_ref[0] = _dot(mem_n, w_ref[...]).astype(BF16)


def _memkv_call(mem, g, w):
    B, M, _ = mem.shape
    return pl.pallas_call(
        _memkv_kernel,
        grid=(B,),
        in_specs=[pl.BlockSpec((1, M, D_MODEL), lambda b: (b, 0, 0)),
                  pl.BlockSpec(g.shape, lambda b: (0, 0)), pl.BlockSpec(w.shape, lambda b: (0, 0))],
        out_specs=pl.BlockSpec((1, M, 2 * D_MODEL), lambda b: (b, 0, 0)),
        out_shape=jax.ShapeDtypeStruct((B, M, 2 * D_MODEL), BF16),
        compiler_params=_params("parallel"),
        name="mem_kv",
    )(mem, g, w)


def _post_kernel(x_ref, od_ref, om_ref, os_ref, wout_ref, gmp_ref, gxp_ref, wxq_ref, kv_ref, wxo_ref,
                 gxo_ref, o_ref):
    mix = (_dot(od_ref[0], wout_ref[0:256, :]) + _dot(om_ref[0], wout_ref[256:512, :])
           + _dot(os_ref[0], wout_ref[512:1024, :]))
    x1 = x_ref[0] + _rms_rows(mix, gmp_ref[...])
    hq = _rms_rows(x1, gxp_ref[...]).astype(BF16)
    q = (_dot(hq, wxq_ref[...]) * (LOG2E / math.sqrt(X_HEAD_DIM))).astype(BF16)
    heads = []
    for hd in range(X_HEADS):
        c0 = hd * X_HEAD_DIM
        s = _nt_dot(q[:, c0:c0 + X_HEAD_DIM], kv_ref[0, :, c0:c0 + X_HEAD_DIM])
        e = jnp.exp2(s - jnp.max(s, axis=-1, keepdims=True))
        den = jnp.sum(e, axis=-1, keepdims=True)
        o = _dot(e.astype(BF16), kv_ref[0, :, D_MODEL + c0:D_MODEL + c0 + X_HEAD_DIM])
        heads.append((o / den).astype(BF16))
    xo = _dot(jnp.concatenate(heads, axis=-1), wxo_ref[...])
    o_ref[0] = x1 + _rms_rows(xo, gxo_ref[...])


def _post_call(x, od, om, os_, wout, gmp, gxp, wxq, kv, wxo, gxo, ts):
    B, S, _ = x.shape
    row = lambda n: pl.BlockSpec((1, ts, n), lambda b, i: (b, i, 0))
    full = lambda a: pl.BlockSpec(a.shape, lambda b, i: (0,) * a.ndim)
    kv_spec = pl.BlockSpec((1,) + kv.shape[1:], lambda b, i: (b, 0, 0))
    return pl.pallas_call(
        _post_kernel,
        grid=(B, S // ts),
        in_specs=[row(D_MODEL), row(256), row(256), row(512), full(wout), full(gmp), full(gxp), full(wxq),
                  kv_spec, full(wxo), full(gxo)],
        out_specs=row(D_MODEL),
        out_shape=jax.ShapeDtypeStruct(x.shape, F32),
        compiler_params=_params("parallel", "parallel"),
        name="mix_cross",
    )(x, od, om, os_, wout, gmp, gxp, wxq, kv, wxo, gxo)


def _ffn_kernel(x_ref, gpre_ref, win_ref, wout_ref, gpost_ref, o_ref):
    x = x_ref[0]
    h = _rms_rows(x, gpre_ref[...]).astype(BF16)
    gate = _dot(h, win_ref[:, 0:D_FF])
    up = _dot(h, win_ref[:, D_FF:2 * D_FF])
    f = _dot((gate * jax.nn.sigmoid(gate) * up).astype(BF16), wout_ref[...])
    o_ref[0] = x + _rms_rows(f, gpost_ref[...])


def _ffn_call(x, gpre, win, wout, gpost, ts):
    B, S, _ = x.shape
    row = pl.BlockSpec((1, ts, D_MODEL), lambda b, i: (b, i, 0))
    full = lambda a: pl.BlockSpec(a.shape, lambda b, i: (0,) * a.ndim)
    return pl.pallas_call(
        _ffn_kernel,
        grid=(B, S // ts),
        in_specs=[row, full(gpre), full(win), full(wout), full(gpost)],
        out_specs=row,
        out_shape=jax.ShapeDtypeStruct(x.shape, F32),
        compiler_params=_params("parallel", "parallel"),
        name="ffn",
    )(x, gpre, win, wout, gpost)


def _prep_weights(w_in, mla_q_norm_g, mla_w_q_up, mla_kv_norm_g, mla_w_kv_up):
    swap = np.concatenate([np.arange(16, 32), np.arange(0, 16)])
    kr = w_in[:, O_BKR:O_CQ]
    zeros64 = jnp.zeros((D_MODEL, 64), w_in.dtype)
    ak = w_in[:, O_AK:O_AV]
    ak_slabs = [a for hd in range(DIFF_HEADS) for a in (ak[:, 64 * hd:64 * (hd + 1)], zeros64)]
    wnat = jnp.concatenate(
        ak_slabs + [w_in[:, O_BCQ:O_BCKV], w_in[:, O_BCKV:O_BKR], kr, kr[:, swap], zeros64,
                    w_in[:, O_CK:O_CV]], axis=1)
    wtr = jnp.concatenate(
        [w_in[:, O_AQ:O_AK], w_in[:, O_AV:O_BCQ], w_in[:, O_CQ:O_CK], w_in[:, O_CV:O_END]], axis=1).T
    qup = mla_w_q_up.reshape(MLA_Q_RANK, MLA_HEADS, MLA_NOPE + MLA_ROPE)
    wqa = jnp.concatenate([qup, jnp.zeros((MLA_Q_RANK, MLA_HEADS, 32), qup.dtype)], axis=2)
    wqa = wqa.reshape(MLA_Q_RANK, MLA_HEADS * LANE).T
    wqb = qup[:, :, MLA_NOPE:][:, :, swap].reshape(MLA_Q_RANK, MLA_HEADS * MLA_ROPE).T
    kvup = mla_w_kv_up.reshape(MLA_KV_RANK, MLA_HEADS, MLA_NOPE + MLA_V)
    wk = jnp.concatenate([kvup[:, :, :MLA_NOPE], jnp.zeros((MLA_KV_RANK, MLA_HEADS, 64), kvup.dtype)], axis=2)
    wk = wk.reshape(MLA_KV_RANK, MLA_HEADS * LANE)
    wvt = kvup[:, :, MLA_NOPE:].reshape(MLA_KV_RANK, MLA_HEADS * MLA_V).T
    place = np.zeros((LANE, MLA_HEADS * LANE), np.float32)
    group_d = np.zeros((DIFF_HEADS * LANE, LANE), np.float32)
    group_m = np.zeros((MLA_HEADS * LANE, LANE), np.float32)
    for hd in range(MLA_HEADS):
        place[np.arange(32), hd * LANE + MLA_NOPE + np.arange(32)] = 1.0
        group_m[hd * LANE + np.arange(MLA_NOPE + MLA_ROPE), DIFF_MAPS + hd] = 1.0
    for mp in range(DIFF_MAPS):
        group_d[(mp // 2) * LANE + (mp % 2) * DIFF_QK + np.arange(DIFF_QK), mp] = 1.0
    return {
        "group_d": jnp.asarray(group_d, BF16), "group_m": jnp.asarray(group_m, BF16),
        "wnat": wnat.astype(BF16), "wtr": wtr.astype(BF16),
        "gq": mla_q_norm_g.reshape(1, -1), "gkv": mla_kv_norm_g.reshape(1, -1),
        "wqa": wqa.astype(BF16), "wqb": wqb.astype(BF16), "wk": wk.astype(BF16), "wvt": wvt.astype(BF16),
        "place": jnp.asarray(place, BF16),
    }


def _rope_tables(positions):
    half = MLA_ROPE // 2
    inv = ROPE_THETA ** (-jnp.arange(half, dtype=F32) / half)
    ang = positions.astype(F32)[..., None] * inv
    cos, sin = jnp.cos(ang), jnp.sin(ang)
    cos2 = jnp.concatenate([cos, cos], axis=-1)
    sin2 = jnp.concatenate([-sin, sin], axis=-1)
    tab = jnp.concatenate([cos2, sin2, jnp.zeros(cos2.shape[:2] + (64,), F32)], axis=-1)
    return {"tab": tab, "cos_t": cos2.swapaxes(1, 2), "sin_t": sin2.swapaxes(1, 2)}


def _key_ranges(positions, nk):
    ktile = positions.reshape(positions.shape[0], nk, -1)
    return jnp.stack([ktile.min(axis=-1), ktile.max(axis=-1)], axis=-1)


def _fast_tiles(qstat, kstat, krange, nq, nk):
    B, nt = qstat.shape[:2]
    assert nt % nq == 0 and nt % nk == 0, (nt, nq, nk)
    qn = qstat[:, :, :N_MAPS, 0].reshape(B, nq, -1, N_MAPS).max(axis=2)
    kn = kstat[:, :, 0, :N_MAPS].reshape(B, nk, -1, N_MAPS).max(axis=2)
    bound2 = qn[:, :, None, :] * kn[:, None, :, :]
    margin = 0.97
    bounded = jnp.all(bound2 <= (margin * LOGIT_LIMIT) ** 2, axis=-1)
    spread = (krange[:, :, 1] - krange[:, :, 0]).astype(F32) * (max(DIFF_SLOPES) * LOG2E)
    return (bounded & (spread <= SPREAD_LIMIT)[:, None, :]).astype(jnp.int32)


def _tile(n, want):
    t = min(n, want)
    assert n % t == 0, (n, t)
    return t


def kernel(x, mem, positions, g_mix_pre, g_mix_post, w_in, diff_lambda, diff_head_g, mla_q_norm_g, mla_w_q_up,
           mla_kv_norm_g, mla_w_kv_up, swa_sinks, w_out, g_x_pre, g_x_mem, g_x_post, w_xq, w_xkv, w_xo,
           g_ffn_pre, g_ffn_post, w_ffn_in, w_ffn_out):
    B, S, _ = x.shape
    depth = w_in.shape[0]
    ts_proj, ts_post, ts_ffn = _tile(S, 512), _tile(S, 512), _tile(S, 256)
    tq, tk, tq_swa = _tile(S, 2048), _tile(S, 512), _tile(S, 512)

    tabs = _rope_tables(positions)
    posq = positions.reshape(B, 1, S)
    posk = positions.reshape(B, S, 1)
    krange = _key_ranges(positions, S // tk)
    row = lambda v: v.reshape(1, -1)

    for l in range(depth):
        w = _prep_weights(w_in[l], mla_q_norm_g[l], mla_w_q_up[l], mla_kv_norm_g[l], mla_w_kv_up[l])
        names = ("qtd", "kd", "vtd", "qtm", "km", "vtm", "qts", "ks", "vts", "qstat", "kstat")
        p = dict(zip(names, _proj_call(x, row(g_mix_pre[l]), w, tabs, ts_proj)))
        linit = jnp.full((1, 1), 0.8 - 0.6 * math.exp(-0.3 * l), F32)
        fast = _fast_tiles(p["qstat"], p["kstat"], krange, S // tq, S // tk)
        od, om = _dense_call(p, fast, krange, posq, posk, diff_lambda[l], diff_head_g[l].reshape(-1, 1), linit, tq, tk)
        os_ = _swa_call(p, posq, posk, swa_sinks[l].reshape(-1, 1), tq_swa)
        kv = _memkv_call(mem, row(g_x_mem[l]), w_xkv[l].astype(BF16))
        x = _post_call(x, od, om, os_, w_out[l].astype(BF16), row(g_mix_post[l]), row(g_x_pre[l]),
                       w_xq[l].astype(BF16), kv, w_xo[l].astype(BF16), row(g_x_post[l]), ts_post)
        x = _ffn_call(x, row(g_ffn_pre[l]), w_ffn_in[l].astype(BF16), w_ffn_out[l].astype(BF16),
                      row(g_ffn_post[l]), ts_ffn)
    return x
```

```python
import math

import jax
import jax.numpy as jnp
import numpy as np
from jax import lax
from jax.experimental import pallas as pl
from jax.experimental.pallas import tpu as pltpu

F32 = jnp.float32
BF16 = jnp.bfloat16

D_MODEL = 1024
HEAD_DIM = 64
DIFF_HEADS = 4
DIFF_QK = 32
DIFF_MAPS = 2 * DIFF_HEADS
MLA_HEADS = 4
MLA_Q_RANK = 256
MLA_KV_RANK = 128
MLA_NOPE = 64
MLA_ROPE = 32
MLA_V = 64
SWA_Q_HEADS = 8
SWA_KV_HEADS = 2
SWA_GROUP = SWA_Q_HEADS // SWA_KV_HEADS
SWA_WINDOW = 128
X_HEADS = 4
X_HEAD_DIM = D_MODEL // X_HEADS
D_FF = 2816
N_ALIBI = DIFF_HEADS + SWA_Q_HEADS
ROPE_THETA = 10000.0
EPS = 1e-6
LOG2E = math.log2(math.e)
NEG_BIG = -1e30

LANE = 128
VMEM_LIMIT = 56 * 1024 * 1024

_OFF = np.cumsum([0, 256, 256, 256, MLA_Q_RANK, MLA_KV_RANK, MLA_ROPE, 512, 128, 128])
(O_AQ, O_AK, O_AV, O_BCQ, O_BCKV, O_BKR, O_CQ, O_CK, O_CV, O_END) = [int(v) for v in _OFF]

ONES_ROWS = 16
VT_ROWS = 64 + ONES_ROWS
AUG_LANES = 3
DIFF_AUG = 64
MLA_AUG = 96
LOGIT_LIMIT = 40.0
SPREAD_LIMIT = 20.0


def _alibi_slopes():
    return [2.0 ** (-8.0 * i / N_ALIBI) for i in range(1, N_ALIBI + 1)]


SWA_SLOPES = _alibi_slopes()[:SWA_Q_HEADS]
DIFF_SLOPES = _alibi_slopes()[SWA_Q_HEADS:]


def _nt_dot(a, b):
    return lax.dot_general(a, b, (((1,), (1,)), ((), ())), preferred_element_type=F32)


def _dot(a, b):
    return jnp.dot(a, b, preferred_element_type=F32)


def _rms_rows(x, g):
    return x * lax.rsqrt(jnp.mean(x * x, axis=-1, keepdims=True) + EPS) * g


def _params(*sem, flags=None):
    return pltpu.CompilerParams(dimension_semantics=sem, vmem_limit_bytes=VMEM_LIMIT, flags=flags)


def _store_values_t(vt_ref, vt, n_heads):
    ones = jnp.ones((ONES_ROWS, vt.shape[1]), BF16)
    for hd in range(n_heads):
        vt_ref[0, hd * VT_ROWS:hd * VT_ROWS + 64, :] = vt[hd * 64:(hd + 1) * 64].astype(BF16)
        vt_ref[0, hd * VT_ROWS + 64:(hd + 1) * VT_ROWS, :] = ones


def _ones_lanes(width, first):
    lane = lax.broadcasted_iota(jnp.int32, (1, width), 1) % LANE
    return jnp.where((lane >= first) & (lane < first + AUG_LANES), 1.0, 0.0)


def _proj_kernel(x_ref, g_ref, wnat_ref, wtr_ref, gq_ref, gkv_ref, wqa_ref, wqb_ref, wk_ref, wvt_ref,
                 pl_ref, gd_ref, gm_ref, tab_ref, cost_ref, sint_ref,
                 qtd_ref, kd_ref, vtd_ref, qtm_ref, km_ref, vtm_ref, qts_ref, ks_ref, vts_ref,
                 qst_ref, kst_ref):
    h = _rms_rows(x_ref[0], g_ref[...]).astype(BF16)
    nat = _dot(h, wnat_ref[...])
    tr = _nt_dot(wtr_ref[...], h)
    ts = nat.shape[0]

    c_diff = LOG2E / math.sqrt(DIFF_QK)
    c_mla = LOG2E / math.sqrt(MLA_NOPE + MLA_ROPE)
    c_swa = LOG2E / math.sqrt(HEAD_DIM)

    qd = tr[0:256] * c_diff
    qtd_ref[0] = qd.astype(BF16)
    _store_values_t(vtd_ref, tr[256:512], DIFF_HEADS)
    qts_ref[0] = (tr[512:1024] * c_swa).astype(BF16)
    vts_ref[0] = tr[1024:1152].astype(BF16)
    kd = nat[:, 0:512]
    kd_ref[0] = (kd + _ones_lanes(512, DIFF_AUG)).astype(BF16)
    ks_ref[0] = nat[:, 1024:1152].astype(BF16)

    cq_n = _rms_rows(nat[:, 512:768], gq_ref[...]).astype(BF16)
    ckv_n = _rms_rows(nat[:, 768:896], gkv_ref[...]).astype(BF16)

    u = nat[:, 896:1024] * tab_ref[0]
    kr = (u + pltpu.roll(u, 96, 1)).astype(BF16)
    km = _dot(ckv_n, wk_ref[...]) + _dot(kr, pl_ref[...])
    km_ref[0] = (km + _ones_lanes(512, MLA_AUG)).astype(BF16)
    _store_values_t(vtm_ref, _nt_dot(wvt_ref[...], ckv_n), MLA_HEADS)

    qa = _nt_dot(wqa_ref[...], cq_n)
    qb = _nt_dot(wqb_ref[...], cq_n)
    cos_t = cost_ref[0]
    sin_t = sint_ref[0]
    for hd in range(MLA_HEADS):
        r0 = hd * LANE
        qtm_ref[0, r0:r0 + 64, :] = (qa[r0:r0 + 64] * c_mla).astype(BF16)
        rope = qa[r0 + 64:r0 + 96] * cos_t + qb[hd * 32:hd * 32 + 32] * sin_t
        qtm_ref[0, r0 + 64:r0 + 96, :] = (rope * c_mla).astype(BF16)
        qtm_ref[0, r0 + 96:r0 + 128, :] = jnp.zeros((32, rope.shape[1]), BF16)

    qn = [jnp.sum((qd * qd).reshape(DIFF_MAPS, DIFF_QK, ts), axis=1)]
    qm = (qa * qa).reshape(MLA_HEADS, LANE, ts)
    qn.append(jnp.sum(qm, axis=1) * (c_mla * c_mla))
    qn.append(jnp.zeros((16 - DIFF_MAPS - MLA_HEADS, ts), F32))
    qmax = jnp.max(jnp.concatenate(qn, axis=0), axis=1, keepdims=True)
    qst_ref[0, 0] = jnp.broadcast_to(qmax, (16, LANE))
    kn = _dot((kd * kd).astype(BF16), gd_ref[...]) + _dot((km * km).astype(BF16), gm_ref[...])
    kst_ref[0, 0] = jnp.broadcast_to(jnp.max(kn, axis=0, keepdims=True), (8, LANE))


def _proj_call(x, g, w, tabs, ts):
    B, S, _ = x.shape
    row = lambda n: pl.BlockSpec((1, ts, n), lambda b, i: (b, i, 0))
    col = lambda n: pl.BlockSpec((1, n, ts), lambda b, i: (b, 0, i))
    full = lambda a: pl.BlockSpec(a.shape, lambda b, i: (0,) * a.ndim)
    consts = [g, w["wnat"], w["wtr"], w["gq"], w["gkv"], w["wqa"], w["wqb"], w["wk"], w["wvt"], w["place"],
              w["group_d"], w["group_m"]]
    n_vt = DIFF_HEADS * VT_ROWS
    nt = S // ts
    stat = lambda r: pl.BlockSpec((1, 1, r, LANE), lambda b, i: (b, i, 0, 0))
    out_shape = [
        jax.ShapeDtypeStruct((B, 256, S), BF16), jax.ShapeDtypeStruct((B, S, 512), BF16),
        jax.ShapeDtypeStruct((B, n_vt, S), BF16), jax.ShapeDtypeStruct((B, 512, S), BF16),
        jax.ShapeDtypeStruct((B, S, 512), BF16), jax.ShapeDtypeStruct((B, n_vt, S), BF16),
        jax.ShapeDtypeStruct((B, 512, S), BF16), jax.ShapeDtypeStruct((B, S, 128), BF16),
        jax.ShapeDtypeStruct((B, 128, S), BF16),
        jax.ShapeDtypeStruct((B, nt, 16, LANE), F32), jax.ShapeDtypeStruct((B, nt, 8, LANE), F32),
    ]
    out_specs = [col(256), row(512), col(n_vt), col(512), row(512), col(n_vt), col(512), row(128), col(128),
                 stat(16), stat(8)]
    return pl.pallas_call(
        _proj_kernel,
        grid=(B, S // ts),
        in_specs=[row(D_MODEL)] + [full(a) for a in consts] + [row(128), col(32), col(32)],
        out_specs=out_specs,
        out_shape=out_shape,
        compiler_params=_params("parallel", "parallel"),
        name="in_proj",
    )(x, *consts, tabs["tab"], tabs["cos_t"], tabs["sin_t"])


STRIP = 256
SLOW_DEPTH = 3
FAST_DEPTH = 2
FAST_KEYS = 512
N_MAPS = DIFF_MAPS + MLA_HEADS


def _split3(x):
    hi = x.astype(BF16).astype(F32)
    mid = (x - hi).astype(BF16).astype(F32)
    lo = (x - hi - mid).astype(BF16).astype(F32)
    return hi, mid, lo


def _dense_kernel(fast_ref, krange_ref, qtd_ref, kd_ref, vtd_ref, qtm_ref, km_ref, vtm_ref, posq_ref, posk_ref,
                  lam_ref, hg_ref, linit_ref, od_ref, om_ref,
                  wq_ref, m_ref, alpha_ref, acc_ref, s_ref):
    b, qi, ki = pl.program_id(0), pl.program_id(1), pl.program_id(2)
    tq = qtd_ref.shape[2]
    n_slots = s_ref.shape[0]

    @pl.when(ki == 0)
    def _init():
        m_ref[...] = jnp.full(m_ref.shape, NEG_BIG, F32)
        acc_ref[...] = jnp.zeros(acc_ref.shape, F32)
        for mp in range(DIFF_MAPS):
            q = qtd_ref[0, mp * DIFF_QK:(mp + 1) * DIFF_QK, :]
            above = jnp.zeros(((mp % 2) * DIFF_QK, tq), BF16)
            below = jnp.zeros((LANE - (mp % 2 + 1) * DIFF_QK, tq), BF16)
            wq_ref[mp] = jnp.concatenate([above, q, below] if mp % 2 else [q, below], axis=0)
        for hd in range(MLA_HEADS):
            wq_ref[DIFF_MAPS + hd] = qtm_ref[0, hd * LANE:(hd + 1) * LANE, :]

    def key_slab(mp, keys):
        if mp < DIFF_MAPS:
            return kd_ref[0, keys, (mp // 2) * LANE:(mp // 2 + 1) * LANE]
        hd = mp - DIFF_MAPS
        return km_ref[0, keys, hd * LANE:(hd + 1) * LANE]

    def values_t(mp, keys):
        if mp < DIFF_MAPS:
            return vtd_ref[0, (mp // 2) * VT_ROWS:(mp // 2 + 1) * VT_ROWS, keys]
        hd = mp - DIFF_MAPS
        return vtm_ref[0, hd * VT_ROWS:(hd + 1) * VT_ROWS, keys]

    def set_shift_rows(shift):
        rows = lax.broadcasted_iota(jnp.int32, (ONES_ROWS, tq), 0)
        for mp in range(N_MAPS):
            hi, mid, lo = _split3(shift[mp:mp + 1, :])
            block = jnp.where(rows == 0, hi, jnp.where(rows == 1, mid, jnp.where(rows == 2, lo, 0.0)))
            first = DIFF_AUG if mp < DIFF_MAPS else MLA_AUG
            wq_ref[mp, first:first + ONES_ROWS, :] = block.astype(BF16)

    def work_items(chunk):
        items = []
        for st in range(tq // STRIP):
            lanes = slice(st * STRIP, (st + 1) * STRIP)
            for kc in range(posk_ref.shape[1] // chunk):
                keys = slice(kc * chunk, (kc + 1) * chunk)
                dist = jnp.abs(posk_ref[0, keys] - posq_ref[0, :, lanes]).astype(F32)
                bias = [dist * (slope * LOG2E) for slope in DIFF_SLOPES]
                items += [(mp, lanes, keys, bias) for mp in range(N_MAPS)]
        return items

    def logits(mp, lanes, keys, bias):
        s = _dot(key_slab(mp, keys), wq_ref[mp, :, lanes])
        return s - bias[mp // 2] if mp < DIFF_MAPS else s

    fast = fast_ref[b, qi, ki] != 0

    @pl.when(fast)
    def _fast_step():
        m_old = m_ref[...]
        kmin, kmax = krange_ref[b, ki, 0], krange_ref[b, ki, 1]
        posq = posq_ref[0]
        outside = jnp.maximum(jnp.maximum(kmin - posq, posq - kmax), 0)
        dmin = (outside + (kmax - kmin)).astype(F32)
        rows = lax.broadcasted_iota(jnp.int32, m_old.shape, 0)
        slope = jnp.zeros(m_old.shape, F32)
        for hd in range(DIFF_HEADS):
            slope = jnp.where(rows // 2 == hd, DIFF_SLOPES[hd] * LOG2E, slope)
        m_eff = jnp.maximum(m_old, -LOGIT_LIMIT - slope * dmin)
        alpha_ref[...] = jnp.exp2(m_old - m_eff)
        m_ref[...] = m_eff
        set_shift_rows(-m_eff)
        items = work_items(FAST_KEYS)
        vals = {}
        for t in range(len(items) + FAST_DEPTH):
            if t < len(items):
                vals[t] = logits(*items[t])
            if t >= FAST_DEPTH:
                mp, lanes, keys, _ = items[t - FAST_DEPTH]
                p = jnp.exp2(vals.pop(t - FAST_DEPTH)).astype(BF16)
                old = acc_ref[mp, :, lanes]
                if keys.start == 0:
                    old = alpha_ref[mp:mp + 1, lanes] * old
                acc_ref[mp, :, lanes] = old + _dot(values_t(mp, keys), p)

    @pl.when(jnp.logical_not(fast))
    def _slow_step():
        set_shift_rows(jnp.zeros(m_ref.shape, F32))
        items = work_items(posk_ref.shape[1])

        def stage_qk(j):
            s_ref[j % n_slots] = logits(*items[j])

        def stage_max(j):
            mp, lanes, _, _ = items[j]
            m_old = m_ref[mp:mp + 1, lanes]
            m_new = jnp.maximum(m_old, jnp.max(s_ref[j % n_slots], axis=0, keepdims=True))
            m_ref[mp:mp + 1, lanes] = m_new
            return m_new, jnp.exp2(m_old - m_new)

        def stage_pv(j, m_new, alpha):
            mp, lanes, keys, _ = items[j]
            p = jnp.exp2((s_ref[j % n_slots] - m_new).astype(BF16))
            acc_ref[mp, :, lanes] = alpha * acc_ref[mp, :, lanes] + _dot(values_t(mp, keys), p)

        stats = {}
        for t in range(len(items) + SLOW_DEPTH):
            if t < len(items):
                stage_qk(t)
            if 1 <= t <= len(items):
                stats[t - 1] = stage_max(t - 1)
            if t >= SLOW_DEPTH:
                stage_pv(t - SLOW_DEPTH, *stats.pop(t - SLOW_DEPTH))

    @pl.when(ki == pl.num_programs(2) - 1)
    def _finish():
        lp = lam_ref[...]
        lam_init = linit_ref[...]
        lam = (jnp.exp(jnp.sum(lp[0:1] * lp[1:2], axis=-1, keepdims=True))
               - jnp.exp(jnp.sum(lp[2:3] * lp[3:4], axis=-1, keepdims=True)) + lam_init)
        def normalised(idx):
            return acc_ref[idx, 0:64, :] / acc_ref[idx, 64:65, :]

        outs = []
        for hd in range(DIFF_HEADS):
            o = normalised(2 * hd) - lam * normalised(2 * hd + 1)
            o = o * lax.rsqrt(jnp.mean(o * o, axis=0, keepdims=True) + EPS) * hg_ref[...]
            outs.append(o * (1.0 - lam_init))
        od_ref[0] = jnp.concatenate(outs, axis=0).T.astype(BF16)
        outs = [normalised(DIFF_MAPS + hd) for hd in range(MLA_HEADS)]
        om_ref[0] = jnp.concatenate(outs, axis=0).T.astype(BF16)


def _dense_call(p, fast, krange, posq, posk, lam, hg, linit, tq, tk):
    B, _, S = p["qtd"].shape
    qcol = lambda n: pl.BlockSpec((1, n, tq), lambda b, i, k, *_: (b, 0, i))
    krow = lambda n: pl.BlockSpec((1, tk, n), lambda b, i, k, *_: (b, k, 0))
    kcol = lambda n: pl.BlockSpec((1, n, tk), lambda b, i, k, *_: (b, 0, k))
    full = lambda a: pl.BlockSpec(a.shape, lambda b, i, k, *_: (0,) * a.ndim)
    n_vt = DIFF_HEADS * VT_ROWS
    grid_spec = pltpu.PrefetchScalarGridSpec(
        num_scalar_prefetch=2,
        grid=(B, S // tq, S // tk),
        in_specs=[qcol(256), krow(512), kcol(n_vt), qcol(512), krow(512), kcol(n_vt),
                  qcol(1), krow(1), full(lam), full(hg), full(linit)],
        out_specs=[pl.BlockSpec((1, tq, 256), lambda b, i, k, *_: (b, i, 0))] * 2,
        scratch_shapes=[pltpu.VMEM((N_MAPS, LANE, tq), BF16),
                        pltpu.VMEM((16, tq), F32), pltpu.VMEM((16, tq), F32),
                        pltpu.VMEM((N_MAPS, VT_ROWS, tq), F32),
                        pltpu.VMEM((SLOW_DEPTH + 2, tk, STRIP), F32)],
    )
    return pl.pallas_call(
        _dense_kernel,
        grid_spec=grid_spec,
        out_shape=[jax.ShapeDtypeStruct((B, S, 256), BF16)] * 2,
        compiler_params=_params("parallel", "parallel", "arbitrary"),
        name="dense_attn",
    )(fast, krange, p["qtd"], p["kd"], p["vtd"], p["qtm"], p["km"], p["vtm"], posq, posk, lam, hg, linit)


SWA_DEPTH = 5


def _swa_kernel(qt_ref, kc_ref, kp_ref, kn_ref, vc_ref, vp_ref, vn_ref,
                posq_ref, pkc_ref, pkp_ref, pkn_ref, sink_ref, o_ref):
    i = pl.program_id(1)
    last = pl.num_programs(1) - 1
    tq = qt_ref.shape[2]
    w = SWA_WINDOW
    n_ext = tq + 2 * w
    k_ext = jnp.concatenate([kp_ref[0], kc_ref[0], kn_ref[0]], axis=0)
    vt_ext = jnp.concatenate([vp_ref[0], vc_ref[0], vn_ref[0]], axis=1)
    pos_ext = jnp.concatenate([pkp_ref[0], pkc_ref[0], pkn_ref[0]], axis=0)
    row = lax.broadcasted_iota(jnp.int32, (n_ext, 1), 0)
    far = 4 * n_ext
    kidx = (row - w + jnp.where(row < w, jnp.where(i == 0, far, 0), 0)
            + jnp.where(row >= tq + w, jnp.where(i == last, far, 0), 0))
    zeros = jnp.zeros((HEAD_DIM, tq), BF16)
    wqs = []
    for hq in range(SWA_Q_HEADS):
        q_h = qt_ref[0, hq * HEAD_DIM:(hq + 1) * HEAD_DIM, :]
        wqs.append(jnp.concatenate([q_h, zeros] if hq // SWA_GROUP == 0 else [zeros, q_h], axis=0))

    def item_logits(r, hq, dist, ok):
        sink = sink_ref[hq:hq + 1, :] * LOG2E
        s = _dot(k_ext[r * w:(r + 3) * w], wqs[hq][:, r * w:(r + 1) * w]) - dist * (SWA_SLOPES[hq] * LOG2E)
        s = jnp.where(ok, s, NEG_BIG)
        m = jnp.maximum(sink, jnp.max(s, axis=0, keepdims=True))
        return s, m, sink

    def item_output(r, hq, s, m, sink):
        kv = hq // SWA_GROUP
        e = jnp.exp2(s - m)
        den = jnp.exp2(sink - m) + jnp.sum(e, axis=0, keepdims=True)
        return _dot(vt_ext[kv * HEAD_DIM:(kv + 1) * HEAD_DIM, r * w:(r + 3) * w], e.astype(BF16)) / den

    items = []
    for r in range(tq // w):
        dist = jnp.abs(pos_ext[r * w:(r + 3) * w] - posq_ref[0, :, r * w:(r + 1) * w]).astype(F32)
        qidx = lax.broadcasted_iota(jnp.int32, (1, w), 1) + r * w
        ok = jnp.abs(kidx[r * w:(r + 3) * w] - qidx) <= w
        items += [(r, hq, dist, ok) for hq in range(SWA_Q_HEADS)]
    outs = {}
    ahead = {}
    for t in range(len(items) + SWA_DEPTH):
        if t < len(items):
            ahead[t] = item_logits(*items[t])
        if t >= SWA_DEPTH:
            r, hq, _, _ = items[t - SWA_DEPTH]
            outs[(hq, r)] = item_output(r, hq, *ahead.pop(t - SWA_DEPTH))
    heads = [jnp.concatenate([outs[(hq, r)] for r in range(tq // w)], axis=1) for hq in range(SWA_Q_HEADS)]
    o_ref[0] = jnp.concatenate(heads, axis=0).T.astype(BF16)


def _swa_call(p, posq, posk, sinks, tq):
    B, _, S = p["qts"].shape
    w = SWA_WINDOW
    r = tq // w
    nblk = S // w
    prev = lambda b, i: (b, jnp.maximum(i * r - 1, 0), 0)
    nxt = lambda b, i: (b, jnp.minimum((i + 1) * r, nblk - 1), 0)
    prev_t = lambda b, i: (b, 0, jnp.maximum(i * r - 1, 0))
    nxt_t = lambda b, i: (b, 0, jnp.minimum((i + 1) * r, nblk - 1))
    cur = lambda b, i: (b, i, 0)
    cur_t = lambda b, i: (b, 0, i)
    return pl.pallas_call(
        _swa_kernel,
        grid=(B, S // tq),
        in_specs=[pl.BlockSpec((1, 512, tq), cur_t),
                  pl.BlockSpec((1, tq, 128), cur), pl.BlockSpec((1, w, 128), prev), pl.BlockSpec((1, w, 128), nxt),
                  pl.BlockSpec((1, 128, tq), cur_t), pl.BlockSpec((1, 128, w), prev_t),
                  pl.BlockSpec((1, 128, w), nxt_t),
                  pl.BlockSpec((1, 1, tq), cur_t),
                  pl.BlockSpec((1, tq, 1), cur), pl.BlockSpec((1, w, 1), prev), pl.BlockSpec((1, w, 1), nxt),
                  pl.BlockSpec(sinks.shape, lambda b, i: (0, 0))],
        out_specs=pl.BlockSpec((1, tq, 512), cur),
        out_shape=jax.ShapeDtypeStruct((B, S, 512), BF16),
        compiler_params=_params("parallel", "parallel"),
        name="swa_attn",
    )(p["qts"], p["ks"], p["ks"], p["ks"], p["vts"], p["vts"], p["vts"],
      posq, posk, posk, posk, sinks)


def _memkv_kernel(mem_ref, g_ref, w_ref, kv_ref):
    mem_n = _rms_rows(mem_ref[0], g_ref[...]).astype(BF16)
    kv_ref[0] = _dot(mem_n, w_ref[...]).astype(BF16)


def _memkv_call(mem, g, w):
    B, M, _ = mem.shape
    return pl.pallas_call(
        _memkv_kernel,
        grid=(B,),
        in_specs=[pl.BlockSpec((1, M, D_MODEL), lambda b: (b, 0, 0)),
                  pl.BlockSpec(g.shape, lambda b: (0, 0)), pl.BlockSpec(w.shape, lambda b: (0, 0))],
        out_specs=pl.BlockSpec((1, M, 2 * D_MODEL), lambda b: (b, 0, 0)),
        out_shape=jax.ShapeDtypeStruct((B, M, 2 * D_MODEL), BF16),
        compiler_params=_params("parallel"),
        name="mem_kv",
    )(mem, g, w)


def _post_kernel(x_ref, od_ref, om_ref, os_ref, wout_ref, gmp_ref, gxp_ref, wxq_ref, kv_ref, wxo_ref,
                 gxo_ref, o_ref):
    mix = (_dot(od_ref[0], wout_ref[0:256, :]) + _dot(om_ref[0], wout_ref[256:512, :])
           + _dot(os_ref[0], wout_ref[512:1024, :]))
    x1 = x_ref[0] + _rms_rows(mix, gmp_ref[...])
    hq = _rms_rows(x1, gxp_ref[...]).astype(BF16)
    q = (_dot(hq, wxq_ref[...]) * (LOG2E / math.sqrt(X_HEAD_DIM))).astype(BF16)
    heads = []
    for hd in range(X_HEADS):
        c0 = hd * X_HEAD_DIM
        s = _nt_dot(q[:, c0:c0 + X_HEAD_DIM], kv_ref[0, :, c0:c0 + X_HEAD_DIM])
        e = jnp.exp2(s - jnp.max(s, axis=-1, keepdims=True))
        den = jnp.sum(e, axis=-1, keepdims=True)
        o = _dot(e.astype(BF16), kv_ref[0, :, D_MODEL + c0:D_MODEL + c0 + X_HEAD_DIM])
        heads.append((o / den).astype(BF16))
    xo = _dot(jnp.concatenate(heads, axis=-1), wxo_ref[...])
    o_ref[0] = x1 + _rms_rows(xo, gxo_ref[...])


def _post_call(x, od, om, os_, wout, gmp, gxp, wxq, kv, wxo, gxo, ts):
    B, S, _ = x.shape
    row = lambda n: pl.BlockSpec((1, ts, n), lambda b, i: (b, i, 0))
    full = lambda a: pl.BlockSpec(a.shape, lambda b, i: (0,) * a.ndim)
    kv_spec = pl.BlockSpec((1,) + kv.shape[1:], lambda b, i: (b, 0, 0))
    return pl.pallas_call(
        _post_kernel,
        grid=(B, S // ts),
        in_specs=[row(D_MODEL), row(256), row(256), row(512), full(wout), full(gmp), full(gxp), full(wxq),
                  kv_spec, full(wxo), full(gxo)],
        out_specs=row(D_MODEL),
        out_shape=jax.ShapeDtypeStruct(x.shape, F32),
        compiler_params=_params("parallel", "parallel"),
        name="mix_cross",
    )(x, od, om, os_, wout, gmp, gxp, wxq, kv, wxo, gxo)


def _ffn_kernel(x_ref, gpre_ref, win_ref, wout_ref, gpost_ref, o_ref):
    x = x_ref[0]
    h = _rms_rows(x, gpre_ref[...]).astype(BF16)
    gate = _dot(h, win_ref[:, 0:D_FF])
    up = _dot(h, win_ref[:, D_FF:2 * D_FF])
    f = _dot((gate * jax.nn.sigmoid(gate) * up).astype(BF16), wout_ref[...])
    o_ref[0] = x + _rms_rows(f, gpost_ref[...])


def _ffn_call(x, gpre, win, wout, gpost, ts):
    B, S, _ = x.shape
    row = pl.BlockSpec((1, ts, D_MODEL), lambda b, i: (b, i, 0))
    full = lambda a: pl.BlockSpec(a.shape, lambda b, i: (0,) * a.ndim)
    return pl.pallas_call(
        _ffn_kernel,
        grid=(B, S // ts),
        in_specs=[row, full(gpre), full(win), full(wout), full(gpost)],
        out_specs=row,
        out_shape=jax.ShapeDtypeStruct(x.shape, F32),
        compiler_params=_params("parallel", "parallel"),
        name="ffn",
    )(x, gpre, win, wout, gpost)


def _prep_weights(w_in, mla_q_norm_g, mla_w_q_up, mla_kv_norm_g, mla_w_kv_up):
    swap = np.concatenate([np.arange(16, 32), np.arange(0, 16)])
    kr = w_in[:, O_BKR:O_CQ]
    zeros64 = jnp.zeros((D_MODEL, 64), w_in.dtype)
    ak = w_in[:, O_AK:O_AV]
    ak_slabs = [a for hd in range(DIFF_HEADS) for a in (ak[:, 64 * hd:64 * (hd + 1)], zeros64)]
    wnat = jnp.concatenate(
        ak_slabs + [w_in[:, O_BCQ:O_BCKV], w_in[:, O_BCKV:O_BKR], kr, kr[:, swap], zeros64,
                    w_in[:, O_CK:O_CV]], axis=1)
    wtr = jnp.concatenate(
        [w_in[:, O_AQ:O_AK], w_in[:, O_AV:O_BCQ], w_in[:, O_CQ:O_CK], w_in[:, O_CV:O_END]], axis=1).T
    qup = mla_w_q_up.reshape(MLA_Q_RANK, MLA_HEADS, MLA_NOPE + MLA_ROPE)
    wqa = jnp.concatenate([qup, jnp.zeros((MLA_Q_RANK, MLA_HEADS, 32), qup.dtype)], axis=2)
    wqa = wqa.reshape(MLA_Q_RANK, MLA_HEADS * LANE).T
    wqb = qup[:, :, MLA_NOPE:][:, :, swap].reshape(MLA_Q_RANK, MLA_HEADS * MLA_ROPE).T
    kvup = mla_w_kv_up.reshape(MLA_KV_RANK, MLA_HEADS, MLA_NOPE + MLA_V)
    wk = jnp.concatenate([kvup[:, :, :MLA_NOPE], jnp.zeros((MLA_KV_RANK, MLA_HEADS, 64), kvup.dtype)], axis=2)
    wk = wk.reshape(MLA_KV_RANK, MLA_HEADS * LANE)
    wvt = kvup[:, :, MLA_NOPE:].reshape(MLA_KV_RANK, MLA_HEADS * MLA_V).T
    place = np.zeros((LANE, MLA_HEADS * LANE), np.float32)
    group_d = np.zeros((DIFF_HEADS * LANE, LANE), np.float32)
    group_m = np.zeros((MLA_HEADS * LANE, LANE), np.float32)
    for hd in range(MLA_HEADS):
        place[np.arange(32), hd * LANE + MLA_NOPE + np.arange(32)] = 1.0
        group_m[hd * LANE + np.arange(MLA_NOPE + MLA_ROPE), DIFF_MAPS + hd] = 1.0
    for mp in range(DIFF_MAPS):
        group_d[(mp // 2) * LANE + (mp % 2) * DIFF_QK + np.arange(DIFF_QK), mp] = 1.0
    return {
        "group_d": jnp.asarray(group_d, BF16), "group_m": jnp.asarray(group_m, BF16),
        "wnat": wnat.astype(BF16), "wtr": wtr.astype(BF16),
        "gq": mla_q_norm_g.reshape(1, -1), "gkv": mla_kv_norm_g.reshape(1, -1),
        "wqa": wqa.astype(BF16), "wqb": wqb.astype(BF16), "wk": wk.astype(BF16), "wvt": wvt.astype(BF16),
        "place": jnp.asarray(place, BF16),
    }


def _rope_tables(positions):
    half = MLA_ROPE // 2
    inv = ROPE_THETA ** (-jnp.arange(half, dtype=F32) / half)
    ang = positions.astype(F32)[..., None] * inv
    cos, sin = jnp.cos(ang), jnp.sin(ang)
    cos2 = jnp.concatenate([cos, cos], axis=-1)
    sin2 = jnp.concatenate([-sin, sin], axis=-1)
    tab = jnp.concatenate([cos2, sin2, jnp.zeros(cos2.shape[:2] + (64,), F32)], axis=-1)
    return {"tab": tab, "cos_t": cos2.swapaxes(1, 2), "sin_t": sin2.swapaxes(1, 2)}


def _key_ranges(positions, nk):
    ktile = positions.reshape(positions.shape[0], nk, -1)
    return jnp.stack([ktile.min(axis=-1), ktile.max(axis=-1)], axis=-1)


def _fast_tiles(qstat, kstat, krange, nq, nk):
    B, nt = qstat.shape[:2]
    assert nt % nq == 0 and nt % nk == 0, (nt, nq, nk)
    qn = qstat[:, :, :N_MAPS, 0].reshape(B, nq, -1, N_MAPS).max(axis=2)
    kn = kstat[:, :, 0, :N_MAPS].reshape(B, nk, -1, N_MAPS).max(axis=2)
    bound2 = qn[:, :, None, :] * kn[:, None, :, :]
    margin = 0.97
    bounded = jnp.all(bound2 <= (margin * LOGIT_LIMIT) ** 2, axis=-1)
    spread = (krange[:, :, 1] - krange[:, :, 0]).astype(F32) * (max(DIFF_SLOPES) * LOG2E)
    return (bounded & (spread <= SPREAD_LIMIT)[:, None, :]).astype(jnp.int32)


def _tile(n, want):
    t = min(n, want)
    assert n % t == 0, (n, t)
    return t


def kernel(x, mem, positions, g_mix_pre, g_mix_post, w_in, diff_lambda, diff_head_g, mla_q_norm_g, mla_w_q_up,
           mla_kv_norm_g, mla_w_kv_up, swa_sinks, w_out, g_x_pre, g_x_mem, g_x_post, w_xq, w_xkv, w_xo,
           g_ffn_pre, g_ffn_post, w_ffn_in, w_ffn_out):
    B, S, _ = x.shape
    depth = w_in.shape[0]
    ts_proj, ts_post, ts_ffn = _tile(S, 512), _tile(S, 512), _tile(S, 256)
    tq, tk, tq_swa = _tile(S, 2048), _tile(S, 512), _tile(S, 512)

    tabs = _rope_tables(positions)
    posq = positions.reshape(B, 1, S)
    posk = positions.reshape(B, S, 1)
    krange = _key_ranges(positions, S // tk)
    row = lambda v: v.reshape(1, -1)

    for l in range(depth):
        w = _prep_weights(w_in[l].astype(BF16), mla_q_norm_g[l], mla_w_q_up[l].astype(BF16),
                          mla_kv_norm_g[l], mla_w_kv_up[l].astype(BF16))
        names = ("qtd", "kd", "vtd", "qtm", "km", "vtm", "qts", "ks", "vts", "qstat", "kstat")
        p = dict(zip(names, _proj_call(x, row(g_mix_pre[l]), w, tabs, ts_proj)))
        linit = jnp.full((1, 1), 0.8 - 0.6 * math.exp(-0.3 * l), F32)
        fast = _fast_tiles(p["qstat"], p["kstat"], krange, S // tq, S // tk)
        od, om = _dense_call(p, fast, krange, posq, posk, diff_lambda[l], diff_head_g[l].reshape(-1, 1), linit, tq, tk)
        os_ = _swa_call(p, posq, posk, swa_sinks[l].reshape(-1, 1), tq_swa)
        kv = _memkv_call(mem, row(g_x_mem[l]), w_xkv[l].astype(BF16))
        x = _post_call(x, od, om, os_, w_out[l].astype(BF16), row(g_mix_post[l]), row(g_x_pre[l]),
                       w_xq[l].astype(BF16), kv, w_xo[l].astype(BF16), row(g_x_post[l]), ts_post)
        x = _ffn_call(x, row(g_ffn_pre[l]), w_ffn_in[l].astype(BF16), w_ffn_out[l].astype(BF16),
                      row(g_ffn_post[l]), ts_ffn)
    return x
```

```python
import math

import jax
import jax.numpy as jnp
import numpy as np
from jax import lax
from jax.experimental import pallas as pl
from jax.experimental.pallas import tpu as pltpu

F32 = jnp.float32
BF16 = jnp.bfloat16

D_MODEL = 1024
HEAD_DIM = 64
DIFF_HEADS = 4
DIFF_QK = 32
DIFF_MAPS = 2 * DIFF_HEADS
MLA_HEADS = 4
MLA_Q_RANK = 256
MLA_KV_RANK = 128
MLA_NOPE = 64
MLA_ROPE = 32
MLA_V = 64
SWA_Q_HEADS = 8
SWA_KV_HEADS = 2
SWA_GROUP = SWA_Q_HEADS // SWA_KV_HEADS
SWA_WINDOW = 128
X_HEADS = 4
X_HEAD_DIM = D_MODEL // X_HEADS
D_FF = 2816
N_ALIBI = DIFF_HEADS + SWA_Q_HEADS
ROPE_THETA = 10000.0
EPS = 1e-6
LOG2E = math.log2(math.e)
NEG_BIG = -1e30

LANE = 128
VMEM_LIMIT = 56 * 1024 * 1024

_OFF = np.cumsum([0, 256, 256, 256, MLA_Q_RANK, MLA_KV_RANK, MLA_ROPE, 512, 128, 128])
(O_AQ, O_AK, O_AV, O_BCQ, O_BCKV, O_BKR, O_CQ, O_CK, O_CV, O_END) = [int(v) for v in _OFF]

ONES_ROWS = 16
VT_ROWS = 64 + ONES_ROWS
AUG_LANES = 3
DIFF_AUG = 64
MLA_AUG = 96
LOGIT_LIMIT = 40.0
SPREAD_LIMIT = 20.0


def _alibi_slopes():
    return [2.0 ** (-8.0 * i / N_ALIBI) for i in range(1, N_ALIBI + 1)]


SWA_SLOPES = _alibi_slopes()[:SWA_Q_HEADS]
DIFF_SLOPES = _alibi_slopes()[SWA_Q_HEADS:]


def _nt_dot(a, b):
    return lax.dot_general(a, b, (((1,), (1,)), ((), ())), preferred_element_type=F32)


def _dot(a, b):
    return jnp.dot(a, b, preferred_element_type=F32)


def _rms_rows(x, g):
    return x * lax.rsqrt(jnp.mean(x * x, axis=-1, keepdims=True) + EPS) * g


def _params(*sem, flags=None):
    return pltpu.CompilerParams(dimension_semantics=sem, vmem_limit_bytes=VMEM_LIMIT, flags=flags)


def _store_values_t(vt_ref, vt, n_heads):
    ones = jnp.ones((ONES_ROWS, vt.shape[1]), BF16)
    for hd in range(n_heads):
        vt_ref[0, hd * VT_ROWS:hd * VT_ROWS + 64, :] = vt[hd * 64:(hd + 1) * 64].astype(BF16)
        vt_ref[0, hd * VT_ROWS + 64:(hd + 1) * VT_ROWS, :] = ones


def _ones_lanes(width, first):
    lane = lax.broadcasted_iota(jnp.int32, (1, width), 1) % LANE
    return jnp.where((lane >= first) & (lane < first + AUG_LANES), 1.0, 0.0)


def _proj_kernel(x_ref, g_ref, wnat_ref, wtr_ref, gq_ref, gkv_ref, wqa_ref, wqb_ref, wk_ref, wvt_ref,
                 pl_ref, gd_ref, gm_ref, tab_ref, cost_ref, sint_ref,
                 qtd_ref, kd_ref, vtd_ref, qtm_ref, km_ref, vtm_ref, qts_ref, ks_ref, vts_ref,
                 qst_ref, kst_ref):
    h = _rms_rows(x_ref[0], g_ref[...]).astype(BF16)
    nat = _dot(h, wnat_ref[...])
    tr = _nt_dot(wtr_ref[...], h)
    ts = nat.shape[0]

    c_diff = LOG2E / math.sqrt(DIFF_QK)
    c_mla = LOG2E / math.sqrt(MLA_NOPE + MLA_ROPE)
    c_swa = LOG2E / math.sqrt(HEAD_DIM)

    qd = tr[0:256] * c_diff
    qtd_ref[0] = qd.astype(BF16)
    _store_values_t(vtd_ref, tr[256:512], DIFF_HEADS)
    qts_ref[0] = (tr[512:1024] * c_swa).astype(BF16)
    vts_ref[0] = tr[1024:1152].astype(BF16)
    kd = nat[:, 0:512]
    kd_ref[0] = (kd + _ones_lanes(512, DIFF_AUG)).astype(BF16)
    ks_ref[0] = nat[:, 1024:1152].astype(BF16)

    cq_n = _rms_rows(nat[:, 512:768], gq_ref[...]).astype(BF16)
    ckv_n = _rms_rows(nat[:, 768:896], gkv_ref[...]).astype(BF16)

    u = nat[:, 896:1024] * tab_ref[0]
    kr = (u + pltpu.roll(u, 96, 1)).astype(BF16)
    km = _dot(ckv_n, wk_ref[...]) + _dot(kr, pl_ref[...])
    km_ref[0] = (km + _ones_lanes(512, MLA_AUG)).astype(BF16)
    _store_values_t(vtm_ref, _nt_dot(wvt_ref[...], ckv_n), MLA_HEADS)

    qa = _nt_dot(wqa_ref[...], cq_n)
    qb = _nt_dot(wqb_ref[...], cq_n)
    cos_t = cost_ref[0]
    sin_t = sint_ref[0]
    for hd in range(MLA_HEADS):
        r0 = hd * LANE
        qtm_ref[0, r0:r0 + 64, :] = (qa[r0:r0 + 64] * c_mla).astype(BF16)
        rope = qa[r0 + 64:r0 + 96] * cos_t + qb[hd * 32:hd * 32 + 32] * sin_t
        qtm_ref[0, r0 + 64:r0 + 96, :] = (rope * c_mla).astype(BF16)
        qtm_ref[0, r0 + 96:r0 + 128, :] = jnp.zeros((32, rope.shape[1]), BF16)

    qn = [jnp.sum((qd * qd).reshape(DIFF_MAPS, DIFF_QK, ts), axis=1)]
    qm = (qa * qa).reshape(MLA_HEADS, LANE, ts)
    qn.append(jnp.sum(qm, axis=1) * (c_mla * c_mla))
    qn.append(jnp.zeros((16 - DIFF_MAPS - MLA_HEADS, ts), F32))
    qmax = jnp.max(jnp.concatenate(qn, axis=0), axis=1, keepdims=True)
    qst_ref[0, 0] = jnp.broadcast_to(qmax, (16, LANE))
    kn = _dot((kd * kd).astype(BF16), gd_ref[...]) + _dot((km * km).astype(BF16), gm_ref[...])
    kst_ref[0, 0] = jnp.broadcast_to(jnp.max(kn, axis=0, keepdims=True), (8, LANE))


def _proj_call(x, g, w, tabs, ts):
    B, S, _ = x.shape
    row = lambda n: pl.BlockSpec((1, ts, n), lambda b, i: (b, i, 0))
    col = lambda n: pl.BlockSpec((1, n, ts), lambda b, i: (b, 0, i))
    full = lambda a: pl.BlockSpec(a.shape, lambda b, i: (0,) * a.ndim)
    consts = [g, w["wnat"], w["wtr"], w["gq"], w["gkv"], w["wqa"], w["wqb"], w["wk"], w["wvt"], w["place"],
              w["group_d"], w["group_m"]]
    n_vt = DIFF_HEADS * VT_ROWS
    nt = S // ts
    stat = lambda r: pl.BlockSpec((1, 1, r, LANE), lambda b, i: (b, i, 0, 0))
    out_shape = [
        jax.ShapeDtypeStruct((B, 256, S), BF16), jax.ShapeDtypeStruct((B, S, 512), BF16),
        jax.ShapeDtypeStruct((B, n_vt, S), BF16), jax.ShapeDtypeStruct((B, 512, S), BF16),
        jax.ShapeDtypeStruct((B, S, 512), BF16), jax.ShapeDtypeStruct((B, n_vt, S), BF16),
        jax.ShapeDtypeStruct((B, 512, S), BF16), jax.ShapeDtypeStruct((B, S, 128), BF16),
        jax.ShapeDtypeStruct((B, 128, S), BF16),
        jax.ShapeDtypeStruct((B, nt, 16, LANE), F32), jax.ShapeDtypeStruct((B, nt, 8, LANE), F32),
    ]
    out_specs = [col(256), row(512), col(n_vt), col(512), row(512), col(n_vt), col(512), row(128), col(128),
                 stat(16), stat(8)]
    return pl.pallas_call(
        _proj_kernel,
        grid=(B, S // ts),
        in_specs=[row(D_MODEL)] + [full(a) for a in consts] + [row(128), col(32), col(32)],
        out_specs=out_specs,
        out_shape=out_shape,
        compiler_params=_params("parallel", "parallel"),
        name="in_proj",
    )(x, *consts, tabs["tab"], tabs["cos_t"], tabs["sin_t"])


STRIP = 256
SLOW_DEPTH = 3
FAST_DEPTH = 2
FAST_KEYS = 512
N_MAPS = DIFF_MAPS + MLA_HEADS


def _split3(x):
    hi = x.astype(BF16).astype(F32)
    mid = (x - hi).astype(BF16).astype(F32)
    lo = (x - hi - mid).astype(BF16).astype(F32)
    return hi, mid, lo


def _dense_kernel(fast_ref, krange_ref, qtd_ref, kd_ref, vtd_ref, qtm_ref, km_ref, vtm_ref, posq_ref, posk_ref,
                  lam_ref, hg_ref, linit_ref, od_ref, om_ref,
                  wq_ref, m_ref, alpha_ref, acc_ref, s_ref):
    b, qi, ki = pl.program_id(0), pl.program_id(1), pl.program_id(2)
    tq = qtd_ref.shape[2]
    n_slots = s_ref.shape[0]

    @pl.when(ki == 0)
    def _init():
        m_ref[...] = jnp.full(m_ref.shape, NEG_BIG, F32)
        acc_ref[...] = jnp.zeros(acc_ref.shape, F32)
        for mp in range(DIFF_MAPS):
            q = qtd_ref[0, mp * DIFF_QK:(mp + 1) * DIFF_QK, :]
            above = jnp.zeros(((mp % 2) * DIFF_QK, tq), BF16)
            below = jnp.zeros((LANE - (mp % 2 + 1) * DIFF_QK, tq), BF16)
            wq_ref[mp] = jnp.concatenate([above, q, below] if mp % 2 else [q, below], axis=0)
        for hd in range(MLA_HEADS):
            wq_ref[DIFF_MAPS + hd] = qtm_ref[0, hd * LANE:(hd + 1) * LANE, :]

    def key_slab(mp, keys):
        if mp < DIFF_MAPS:
            return kd_ref[0, keys, (mp // 2) * LANE:(mp // 2 + 1) * LANE]
        hd = mp - DIFF_MAPS
        return km_ref[0, keys, hd * LANE:(hd + 1) * LANE]

    def values_t(mp, keys):
        if mp < DIFF_MAPS:
            return vtd_ref[0, (mp // 2) * VT_ROWS:(mp // 2 + 1) * VT_ROWS, keys]
        hd = mp - DIFF_MAPS
        return vtm_ref[0, hd * VT_ROWS:(hd + 1) * VT_ROWS, keys]

    def set_shift_rows(shift):
        rows = lax.broadcasted_iota(jnp.int32, (ONES_ROWS, tq), 0)
        for mp in range(N_MAPS):
            hi, mid, lo = _split3(shift[mp:mp + 1, :])
            block = jnp.where(rows == 0, hi, jnp.where(rows == 1, mid, jnp.where(rows == 2, lo, 0.0)))
            first = DIFF_AUG if mp < DIFF_MAPS else MLA_AUG
            wq_ref[mp, first:first + ONES_ROWS, :] = block.astype(BF16)

    def work_items(chunk):
        items = []
        for st in range(tq // STRIP):
            lanes = slice(st * STRIP, (st + 1) * STRIP)
            for kc in range(posk_ref.shape[1] // chunk):
                keys = slice(kc * chunk, (kc + 1) * chunk)
                dist = jnp.abs(posk_ref[0, keys] - posq_ref[0, :, lanes]).astype(F32)
                bias = [dist * (slope * LOG2E) for slope in DIFF_SLOPES]
                items += [(mp, lanes, keys, bias) for mp in range(N_MAPS)]
        return items

    def logits(mp, lanes, keys, bias):
        s = _dot(key_slab(mp, keys), wq_ref[mp, :, lanes])
        return s - bias[mp // 2] if mp < DIFF_MAPS else s

    fast = fast_ref[b, qi, ki] != 0

    @pl.when(fast)
    def _fast_step():
        m_old = m_ref[...]
        kmin, kmax = krange_ref[b, ki, 0], krange_ref[b, ki, 1]
        posq = posq_ref[0]
        outside = jnp.maximum(jnp.maximum(kmin - posq, posq - kmax), 0)
        dmin = (outside + (kmax - kmin)).astype(F32)
        rows = lax.broadcasted_iota(jnp.int32, m_old.shape, 0)
        slope = jnp.zeros(m_old.shape, F32)
        for hd in range(DIFF_HEADS):
            slope = jnp.where(rows // 2 == hd, DIFF_SLOPES[hd] * LOG2E, slope)
        m_eff = jnp.maximum(m_old, -LOGIT_LIMIT - slope * dmin)
        alpha_ref[...] = jnp.exp2(m_old - m_eff)
        m_ref[...] = m_eff
        set_shift_rows(-m_eff)
        items = work_items(FAST_KEYS)
        vals = {}
        for t in range(len(items) + FAST_DEPTH):
            if t < len(items):
                vals[t] = logits(*items[t])
            if t >= FAST_DEPTH:
                mp, lanes, keys, _ = items[t - FAST_DEPTH]
                p = jnp.exp2(vals.pop(t - FAST_DEPTH)).astype(BF16)
                old = acc_ref[mp, :, lanes]
                if keys.start == 0:
                    old = alpha_ref[mp:mp + 1, lanes] * old
                acc_ref[mp, :, lanes] = old + _dot(values_t(mp, keys), p)

    @pl.when(jnp.logical_not(fast))
    def _slow_step():
        set_shift_rows(jnp.zeros(m_ref.shape, F32))
        items = work_items(posk_ref.shape[1])

        def stage_qk(j):
            s_ref[j % n_slots] = logits(*items[j])

        def stage_max(j):
            mp, lanes, _, _ = items[j]
            m_old = m_ref[mp:mp + 1, lanes]
            m_new = jnp.maximum(m_old, jnp.max(s_ref[j % n_slots], axis=0, keepdims=True))
            m_ref[mp:mp + 1, lanes] = m_new
            return m_new, jnp.exp2(m_old - m_new)

        def stage_pv(j, m_new, alpha):
            mp, lanes, keys, _ = items[j]
            p = jnp.exp2((s_ref[j % n_slots] - m_new).astype(BF16))
            acc_ref[mp, :, lanes] = alpha * acc_ref[mp, :, lanes] + _dot(values_t(mp, keys), p)

        stats = {}
        for t in range(len(items) + SLOW_DEPTH):
            if t < len(items):
                stage_qk(t)
            if 1 <= t <= len(items):
                stats[t - 1] = stage_max(t - 1)
            if t >= SLOW_DEPTH:
                stage_pv(t - SLOW_DEPTH, *stats.pop(t - SLOW_DEPTH))

    @pl.when(ki == pl.num_programs(2) - 1)
    def _finish():
        lp = lam_ref[...]
        lam_init = linit_ref[...]
        lam = (jnp.exp(jnp.sum(lp[0:1] * lp[1:2], axis=-1, keepdims=True))
               - jnp.exp(jnp.sum(lp[2:3] * lp[3:4], axis=-1, keepdims=True)) + lam_init)
        def normalised(idx):
            return acc_ref[idx, 0:64, :] / acc_ref[idx, 64:65, :]

        outs = []
        for hd in range(DIFF_HEADS):
            o = normalised(2 * hd) - lam * normalised(2 * hd + 1)
            o = o * lax.rsqrt(jnp.mean(o * o, axis=0, keepdims=True) + EPS) * hg_ref[...]
            outs.append(o * (1.0 - lam_init))
        od_ref[0] = jnp.concatenate(outs, axis=0).T.astype(BF16)
        outs = [normalised(DIFF_MAPS + hd) for hd in range(MLA_HEADS)]
        om_ref[0] = jnp.concatenate(outs, axis=0).T.astype(BF16)


def _dense_call(p, fast, krange, posq, posk, lam, hg, linit, tq, tk):
    B, _, S = p["qtd"].shape
    qcol = lambda n: pl.BlockSpec((1, n, tq), lambda b, i, k, *_: (b, 0, i))
    krow = lambda n: pl.BlockSpec((1, tk, n), lambda b, i, k, *_: (b, k, 0))
    kcol = lambda n: pl.BlockSpec((1, n, tk), lambda b, i, k, *_: (b, 0, k))
    full = lambda a: pl.BlockSpec(a.shape, lambda b, i, k, *_: (0,) * a.ndim)
    n_vt = DIFF_HEADS * VT_ROWS
    grid_spec = pltpu.PrefetchScalarGridSpec(
        num_scalar_prefetch=2,
        grid=(B, S // tq, S // tk),
        in_specs=[qcol(256), krow(512), kcol(n_vt), qcol(512), krow(512), kcol(n_vt),
                  qcol(1), krow(1), full(lam), full(hg), full(linit)],
        out_specs=[pl.BlockSpec((1, tq, 256), lambda b, i, k, *_: (b, i, 0))] * 2,
        scratch_shapes=[pltpu.VMEM((N_MAPS, LANE, tq), BF16),
                        pltpu.VMEM((16, tq), F32), pltpu.VMEM((16, tq), F32),
                        pltpu.VMEM((N_MAPS, VT_ROWS, tq), F32),
                        pltpu.VMEM((SLOW_DEPTH + 2, tk, STRIP), F32)],
    )
    return pl.pallas_call(
        _dense_kernel,
        grid_spec=grid_spec,
        out_shape=[jax.ShapeDtypeStruct((B, S, 256), BF16)] * 2,
        compiler_params=_params("parallel", "parallel", "arbitrary"),
        name="dense_attn",
    )(fast, krange, p["qtd"], p["kd"], p["vtd"], p["qtm"], p["km"], p["vtm"], posq, posk, lam, hg, linit)


SWA_DEPTH = 5


def _swa_kernel(qt_ref, kc_ref, kp_ref, kn_ref, vc_ref, vp_ref, vn_ref,
                posq_ref, pkc_ref, pkp_ref, pkn_ref, sink_ref, o_ref):
    i = pl.program_id(1)
    last = pl.num_programs(1) - 1
    tq = qt_ref.shape[2]
    w = SWA_WINDOW
    n_ext = tq + 2 * w
    k_ext = jnp.concatenate([kp_ref[0], kc_ref[0], kn_ref[0]], axis=0)
    vt_ext = jnp.concatenate([vp_ref[0], vc_ref[0], vn_ref[0]], axis=1)
    pos_ext = jnp.concatenate([pkp_ref[0], pkc_ref[0], pkn_ref[0]], axis=0)
    row = lax.broadcasted_iota(jnp.int32, (n_ext, 1), 0)
    far = 4 * n_ext
    kidx = (row - w + jnp.where(row < w, jnp.where(i == 0, far, 0), 0)
            + jnp.where(row >= tq + w, jnp.where(i == last, far, 0), 0))
    zeros = jnp.zeros((HEAD_DIM, tq), BF16)
    wqs = []
    for hq in range(SWA_Q_HEADS):
        q_h = qt_ref[0, hq * HEAD_DIM:(hq + 1) * HEAD_DIM, :]
        wqs.append(jnp.concatenate([q_h, zeros] if hq // SWA_GROUP == 0 else [zeros, q_h], axis=0))

    def item_logits(r, hq, dist, ok):
        sink = sink_ref[hq:hq + 1, :] * LOG2E
        s = _dot(k_ext[r * w:(r + 3) * w], wqs[hq][:, r * w:(r + 1) * w]) - dist * (SWA_SLOPES[hq] * LOG2E)
        s = jnp.where(ok, s, NEG_BIG)
        m = jnp.maximum(sink, jnp.max(s, axis=0, keepdims=True))
        return s, m, sink

    def item_output(r, hq, s, m, sink):
        kv = hq // SWA_GROUP
        e = jnp.exp2(s - m)
        den = jnp.exp2(sink - m) + jnp.sum(e, axis=0, keepdims=True)
        return _dot(vt_ext[kv * HEAD_DIM:(kv + 1) * HEAD_DIM, r * w:(r + 3) * w], e.astype(BF16)) / den

    items = []
    for r in range(tq // w):
        dist = jnp.abs(pos_ext[r * w:(r + 3) * w] - posq_ref[0, :, r * w:(r + 1) * w]).astype(F32)
        qidx = lax.broadcasted_iota(jnp.int32, (1, w), 1) + r * w
        ok = jnp.abs(kidx[r * w:(r + 3) * w] - qidx) <= w
        items += [(r, hq, dist, ok) for hq in range(SWA_Q_HEADS)]
    outs = {}
    ahead = {}
    for t in range(len(items) + SWA_DEPTH):
        if t < len(items):
            ahead[t] = item_logits(*items[t])
        if t >= SWA_DEPTH:
            r, hq, _, _ = items[t - SWA_DEPTH]
            outs[(hq, r)] = item_output(r, hq, *ahead.pop(t - SWA_DEPTH))
    heads = [jnp.concatenate([outs[(hq, r)] for r in range(tq // w)], axis=1) for hq in range(SWA_Q_HEADS)]
    o_ref[0] = jnp.concatenate(heads, axis=0).T.astype(BF16)


def _swa_call(p, posq, posk, sinks, tq):
    B, _, S = p["qts"].shape
    w = SWA_WINDOW
    r = tq // w
    nblk = S // w
    prev = lambda b, i: (b, jnp.maximum(i * r - 1, 0), 0)
    nxt = lambda b, i: (b, jnp.minimum((i + 1) * r, nblk - 1), 0)
    prev_t = lambda b, i: (b, 0, jnp.maximum(i * r - 1, 0))
    nxt_t = lambda b, i: (b, 0, jnp.minimum((i + 1) * r, nblk - 1))
    cur = lambda b, i: (b, i, 0)
    cur_t = lambda b, i: (b, 0, i)
    return pl.pallas_call(
        _swa_kernel,
        grid=(B, S // tq),
        in_specs=[pl.BlockSpec((1, 512, tq), cur_t),
                  pl.BlockSpec((1, tq, 128), cur), pl.BlockSpec((1, w, 128), prev), pl.BlockSpec((1, w, 128), nxt),
                  pl.BlockSpec((1, 128, tq), cur_t), pl.BlockSpec((1, 128, w), prev_t),
                  pl.BlockSpec((1, 128, w), nxt_t),
                  pl.BlockSpec((1, 1, tq), cur_t),
                  pl.BlockSpec((1, tq, 1), cur), pl.BlockSpec((1, w, 1), prev), pl.BlockSpec((1, w, 1), nxt),
                  pl.BlockSpec(sinks.shape, lambda b, i: (0, 0))],
        out_specs=pl.BlockSpec((1, tq, 512), cur),
        out_shape=jax.ShapeDtypeStruct((B, S, 512), BF16),
        compiler_params=_params("parallel", "parallel"),
        name="swa_attn",
    )(p["qts"], p["ks"], p["ks"], p["ks"], p["vts"], p["vts"], p["vts"],
      posq, posk, posk, posk, sinks)


def _memkv_kernel(mem_ref, g_ref, w_ref, kv_ref):
    mem_n = _rms_rows(mem_ref[0], g_ref[...]).astype(BF16)
    kv_ref[0] = _dot(mem_n, w_ref[...]).astype(BF16)


def _memkv_call(mem, g, w):
    B, M, _ = mem.shape
    return pl.pallas_call(
        _memkv_kernel,
        grid=(B,),
        in_specs=[pl.BlockSpec((1, M, D_MODEL), lambda b: (b, 0, 0)),
                  pl.BlockSpec(g.shape, lambda b: (0, 0)), pl.BlockSpec(w.shape, lambda b: (0, 0))],
        out_specs=pl.BlockSpec((1, M, 2 * D_MODEL), lambda b: (b, 0, 0)),
        out_shape=jax.ShapeDtypeStruct((B, M, 2 * D_MODEL), BF16),
        compiler_params=_params("parallel"),
        name="mem_kv",
    )(mem, g, w)


def _tail_kernel(x_ref, od_ref, om_ref, os_ref, wout_ref, gmp_ref, gxp_ref, wxq_ref, kv_ref, wxo_ref, gxo_ref,
                 gfi_ref, wfi_ref, wfo_ref, gfo_ref, o_ref):
    mix = (_dot(od_ref[0], wout_ref[0:256, :]) + _dot(om_ref[0], wout_ref[256:512, :])
           + _dot(os_ref[0], wout_ref[512:1024, :]))
    x1 = x_ref[0] + _rms_rows(mix, gmp_ref[...])
    hq = _rms_rows(x1, gxp_ref[...]).astype(BF16)
    q = (_dot(hq, wxq_ref[...]) * (LOG2E / math.sqrt(X_HEAD_DIM))).astype(BF16)
    heads = []
    for hd in range(X_HEADS):
        c0 = hd * X_HEAD_DIM
        s = _nt_dot(q[:, c0:c0 + X_HEAD_DIM], kv_ref[0, :, c0:c0 + X_HEAD_DIM])
        e = jnp.exp2(s - jnp.max(s, axis=-1, keepdims=True))
        den = jnp.sum(e, axis=-1, keepdims=True)
        o = _dot(e.astype(BF16), kv_ref[0, :, D_MODEL + c0:D_MODEL + c0 + X_HEAD_DIM])
        heads.append((o / den).astype(BF16))
    xo = _dot(jnp.concatenate(heads, axis=-1), wxo_ref[...])
    x2 = x1 + _rms_rows(xo, gxo_ref[...])
    h = _rms_rows(x2, gfi_ref[...]).astype(BF16)
    gate = _dot(h, wfi_ref[:, 0:D_FF])
    up = _dot(h, wfi_ref[:, D_FF:2 * D_FF])
    f = _dot((gate * jax.nn.sigmoid(gate) * up).astype(BF16), wfo_ref[...])
    o_ref[0] = x2 + _rms_rows(f, gfo_ref[...])


def _tail_call(x, od, om, os_, kv, consts, ts):
    B, S, _ = x.shape
    wout, gmp, gxp, wxq, wxo, gxo, gfi, wfi, wfo, gfo = consts
    row = lambda n: pl.BlockSpec((1, ts, n), lambda b, i: (b, i, 0))
    full = lambda a: pl.BlockSpec(a.shape, lambda b, i: (0,) * a.ndim, pipeline_mode=pl.Buffered(1))
    kv_spec = pl.BlockSpec((1,) + kv.shape[1:], lambda b, i: (b, 0, 0))
    return pl.pallas_call(
        _tail_kernel,
        grid=(B, S // ts),
        in_specs=[row(D_MODEL), row(256), row(256), row(512), full(wout), full(gmp), full(gxp), full(wxq),
                  kv_spec, full(wxo), full(gxo), full(gfi), full(wfi), full(wfo), full(gfo)],
        out_specs=row(D_MODEL),
        out_shape=jax.ShapeDtypeStruct(x.shape, F32),
        compiler_params=_params("parallel", "parallel"),
        name="layer_tail",
    )(x, od, om, os_, wout, gmp, gxp, wxq, kv, wxo, gxo, gfi, wfi, wfo, gfo)


def _prep_weights(w_in, mla_q_norm_g, mla_w_q_up, mla_kv_norm_g, mla_w_kv_up):
    swap = np.concatenate([np.arange(16, 32), np.arange(0, 16)])
    kr = w_in[:, O_BKR:O_CQ]
    zeros64 = jnp.zeros((D_MODEL, 64), w_in.dtype)
    ak = w_in[:, O_AK:O_AV]
    ak_slabs = [a for hd in range(DIFF_HEADS) for a in (ak[:, 64 * hd:64 * (hd + 1)], zeros64)]
    wnat = jnp.concatenate(
        ak_slabs + [w_in[:, O_BCQ:O_BCKV], w_in[:, O_BCKV:O_BKR], kr, kr[:, swap], zeros64,
                    w_in[:, O_CK:O_CV]], axis=1)
    wtr = jnp.concatenate(
        [w_in[:, O_AQ:O_AK], w_in[:, O_AV:O_BCQ], w_in[:, O_CQ:O_CK], w_in[:, O_CV:O_END]], axis=1).T
    qup = mla_w_q_up.reshape(MLA_Q_RANK, MLA_HEADS, MLA_NOPE + MLA_ROPE)
    wqa = jnp.concatenate([qup, jnp.zeros((MLA_Q_RANK, MLA_HEADS, 32), qup.dtype)], axis=2)
    wqa = wqa.reshape(MLA_Q_RANK, MLA_HEADS * LANE).T
    wqb = qup[:, :, MLA_NOPE:][:, :, swap].reshape(MLA_Q_RANK, MLA_HEADS * MLA_ROPE).T
    kvup = mla_w_kv_up.reshape(MLA_KV_RANK, MLA_HEADS, MLA_NOPE + MLA_V)
    wk = jnp.concatenate([kvup[:, :, :MLA_NOPE], jnp.zeros((MLA_KV_RANK, MLA_HEADS, 64), kvup.dtype)], axis=2)
    wk = wk.reshape(MLA_KV_RANK, MLA_HEADS * LANE)
    wvt = kvup[:, :, MLA_NOPE:].reshape(MLA_KV_RANK, MLA_HEADS * MLA_V).T
    place = np.zeros((LANE, MLA_HEADS * LANE), np.float32)
    group_d = np.zeros((DIFF_HEADS * LANE, LANE), np.float32)
    group_m = np.zeros((MLA_HEADS * LANE, LANE), np.float32)
    for hd in range(MLA_HEADS):
        place[np.arange(32), hd * LANE + MLA_NOPE + np.arange(32)] = 1.0
        group_m[hd * LANE + np.arange(MLA_NOPE + MLA_ROPE), DIFF_MAPS + hd] = 1.0
    for mp in range(DIFF_MAPS):
        group_d[(mp // 2) * LANE + (mp % 2) * DIFF_QK + np.arange(DIFF_QK), mp] = 1.0
    return {
        "group_d": jnp.asarray(group_d, BF16), "group_m": jnp.asarray(group_m, BF16),
        "wnat": wnat.astype(BF16), "wtr": wtr.astype(BF16),
        "gq": mla_q_norm_g.reshape(1, -1), "gkv": mla_kv_norm_g.reshape(1, -1),
        "wqa": wqa.astype(BF16), "wqb": wqb.astype(BF16), "wk": wk.astype(BF16), "wvt": wvt.astype(BF16),
        "place": jnp.asarray(place, BF16),
    }


def _rope_tables(positions):
    half = MLA_ROPE // 2
    inv = ROPE_THETA ** (-jnp.arange(half, dtype=F32) / half)
    ang = positions.astype(F32)[..., None] * inv
    cos, sin = jnp.cos(ang), jnp.sin(ang)
    cos2 = jnp.concatenate([cos, cos], axis=-1)
    sin2 = jnp.concatenate([-sin, sin], axis=-1)
    tab = jnp.concatenate([cos2, sin2, jnp.zeros(cos2.shape[:2] + (64,), F32)], axis=-1)
    return {"tab": tab, "cos_t": cos2.swapaxes(1, 2), "sin_t": sin2.swapaxes(1, 2)}


def _key_ranges(positions, nk):
    ktile = positions.reshape(positions.shape[0], nk, -1)
    return jnp.stack([ktile.min(axis=-1), ktile.max(axis=-1)], axis=-1)


def _fast_tiles(qstat, kstat, krange, nq, nk):
    B, nt = qstat.shape[:2]
    assert nt % nq == 0 and nt % nk == 0, (nt, nq, nk)
    qn = qstat[:, :, :N_MAPS, 0].reshape(B, nq, -1, N_MAPS).max(axis=2)
    kn = kstat[:, :, 0, :N_MAPS].reshape(B, nk, -1, N_MAPS).max(axis=2)
    bound2 = qn[:, :, None, :] * kn[:, None, :, :]
    margin = 0.97
    bounded = jnp.all(bound2 <= (margin * LOGIT_LIMIT) ** 2, axis=-1)
    spread = (krange[:, :, 1] - krange[:, :, 0]).astype(F32) * (max(DIFF_SLOPES) * LOG2E)
    return (bounded & (spread <= SPREAD_LIMIT)[:, None, :]).astype(jnp.int32)


def _tile(n, want):
    t = min(n, want)
    assert n % t == 0, (n, t)
    return t


def kernel(x, mem, positions, g_mix_pre, g_mix_post, w_in, diff_lambda, diff_head_g, mla_q_norm_g, mla_w_q_up,
           mla_kv_norm_g, mla_w_kv_up, swa_sinks, w_out, g_x_pre, g_x_mem, g_x_post, w_xq, w_xkv, w_xo,
           g_ffn_pre, g_ffn_post, w_ffn_in, w_ffn_out):
    B, S, _ = x.shape
    depth = w_in.shape[0]
    ts_proj, ts_tail = _tile(S, 512), _tile(S, 256)
    tq, tk, tq_swa = _tile(S, 2048), _tile(S, 512), _tile(S, 512)

    tabs = _rope_tables(positions)
    posq = positions.reshape(B, 1, S)
    posk = positions.reshape(B, S, 1)
    krange = _key_ranges(positions, S // tk)
    row = lambda v: v.reshape(1, -1)

    for l in range(depth):
        w = _prep_weights(w_in[l].astype(BF16), mla_q_norm_g[l], mla_w_q_up[l].astype(BF16),
                          mla_kv_norm_g[l], mla_w_kv_up[l].astype(BF16))
        names = ("qtd", "kd", "vtd", "qtm", "km", "vtm", "qts", "ks", "vts", "qstat", "kstat")
        p = dict(zip(names, _proj_call(x, row(g_mix_pre[l]), w, tabs, ts_proj)))
        linit = jnp.full((1, 1), 0.8 - 0.6 * math.exp(-0.3 * l), F32)
        fast = _fast_tiles(p["qstat"], p["kstat"], krange, S // tq, S // tk)
        od, om = _dense_call(p, fast, krange, posq, posk, diff_lambda[l], diff_head_g[l].reshape(-1, 1), linit, tq, tk)
        os_ = _swa_call(p, posq, posk, swa_sinks[l].reshape(-1, 1), tq_swa)
        kv = _memkv_call(mem, row(g_x_mem[l]), w_xkv[l].astype(BF16))
        consts = (w_out[l].astype(BF16), row(g_mix_post[l]), row(g_x_pre[l]), w_xq[l].astype(BF16),
                  w_xo[l].astype(BF16), row(g_x_post[l]), row(g_ffn_pre[l]), w_ffn_in[l].astype(BF16),
                  w_ffn_out[l].astype(BF16), row(g_ffn_post[l]))
        x = _tail_call(x, od, om, os_, kv, consts, ts_tail)
    return x
```

```python
import math

import jax
import jax.numpy as jnp
import numpy as np
from jax import lax
from jax.experimental import pallas as pl
from jax.experimental.pallas import tpu as pltpu

F32 = jnp.float32
BF16 = jnp.bfloat16

D_MODEL = 1024
HEAD_DIM = 64
DIFF_HEADS = 4
DIFF_QK = 32
DIFF_MAPS = 2 * DIFF_HEADS
MLA_HEADS = 4
MLA_Q_RANK = 256
MLA_KV_RANK = 128
MLA_NOPE = 64
MLA_ROPE = 32
MLA_V = 64
SWA_Q_HEADS = 8
SWA_KV_HEADS = 2
SWA_GROUP = SWA_Q_HEADS // SWA_KV_HEADS
SWA_WINDOW = 128
X_HEADS = 4
X_HEAD_DIM = D_MODEL // X_HEADS
D_FF = 2816
N_ALIBI = DIFF_HEADS + SWA_Q_HEADS
ROPE_THETA = 10000.0
EPS = 1e-6
LOG2E = math.log2(math.e)
NEG_BIG = -1e30

LANE = 128
VMEM_LIMIT = 56 * 1024 * 1024

_OFF = np.cumsum([0, 256, 256, 256, MLA_Q_RANK, MLA_KV_RANK, MLA_ROPE, 512, 128, 128])
(O_AQ, O_AK, O_AV, O_BCQ, O_BCKV, O_BKR, O_CQ, O_CK, O_CV, O_END) = [int(v) for v in _OFF]

ONES_ROWS = 16
VT_ROWS = 64 + ONES_ROWS
AUG_LANES = 3
DIFF_AUG = 64
MLA_AUG = 96
LOGIT_LIMIT = 40.0
SPREAD_LIMIT = 20.0


def _alibi_slopes():
    return [2.0 ** (-8.0 * i / N_ALIBI) for i in range(1, N_ALIBI + 1)]


SWA_SLOPES = _alibi_slopes()[:SWA_Q_HEADS]
DIFF_SLOPES = _alibi_slopes()[SWA_Q_HEADS:]


def _nt_dot(a, b):
    return lax.dot_general(a, b, (((1,), (1,)), ((), ())), preferred_element_type=F32)


def _dot(a, b):
    return jnp.dot(a, b, preferred_element_type=F32)


def _rms_rows(x, g):
    return x * lax.rsqrt(jnp.mean(x * x, axis=-1, keepdims=True) + EPS) * g


def _params(*sem, flags=None):
    return pltpu.CompilerParams(dimension_semantics=sem, vmem_limit_bytes=VMEM_LIMIT, flags=flags)


def _store_values_t(vt_ref, vt, n_heads):
    ones = jnp.ones((ONES_ROWS, vt.shape[1]), BF16)
    for hd in range(n_heads):
        vt_ref[0, hd * VT_ROWS:hd * VT_ROWS + 64, :] = vt[hd * 64:(hd + 1) * 64].astype(BF16)
        vt_ref[0, hd * VT_ROWS + 64:(hd + 1) * VT_ROWS, :] = ones


def _ones_lanes(width, first):
    lane = lax.broadcasted_iota(jnp.int32, (1, width), 1) % LANE
    return jnp.where((lane >= first) & (lane < first + AUG_LANES), 1.0, 0.0)


def _proj_kernel(x_ref, g_ref, wnat_ref, wtr_ref, gq_ref, gkv_ref, wqa_ref, wqb_ref, wk_ref, wvt_ref,
                 pl_ref, gd_ref, gm_ref, tab_ref, cost_ref, sint_ref,
                 qtd_ref, kd_ref, vtd_ref, qtm_ref, km_ref, vtm_ref, qts_ref, ks_ref, vts_ref,
                 qst_ref, kst_ref):
    h = _rms_rows(x_ref[0], g_ref[...]).astype(BF16)
    nat = _dot(h, wnat_ref[...])
    tr = _nt_dot(wtr_ref[...], h)
    ts = nat.shape[0]

    c_diff = LOG2E / math.sqrt(DIFF_QK)
    c_mla = LOG2E / math.sqrt(MLA_NOPE + MLA_ROPE)
    c_swa = LOG2E / math.sqrt(HEAD_DIM)

    qd = tr[0:256] * c_diff
    qtd_ref[0] = qd.astype(BF16)
    _store_values_t(vtd_ref, tr[256:512], DIFF_HEADS)
    qts_ref[0] = (tr[512:1024] * c_swa).astype(BF16)
    vts_ref[0] = tr[1024:1152].astype(BF16)
    kd = nat[:, 0:512]
    kd_ref[0] = (kd + _ones_lanes(512, DIFF_AUG)).astype(BF16)
    ks_ref[0] = nat[:, 1024:1152].astype(BF16)

    cq_n = _rms_rows(nat[:, 512:768], gq_ref[...]).astype(BF16)
    ckv_n = _rms_rows(nat[:, 768:896], gkv_ref[...]).astype(BF16)

    u = nat[:, 896:1024] * tab_ref[0]
    kr = (u + pltpu.roll(u, 96, 1)).astype(BF16)
    km = _dot(ckv_n, wk_ref[...]) + _dot(kr, pl_ref[...])
    km_ref[0] = (km + _ones_lanes(512, MLA_AUG)).astype(BF16)
    _store_values_t(vtm_ref, _nt_dot(wvt_ref[...], ckv_n), MLA_HEADS)

    qa = _nt_dot(wqa_ref[...], cq_n)
    qb = _nt_dot(wqb_ref[...], cq_n)
    cos_t = cost_ref[0]
    sin_t = sint_ref[0]
    for hd in range(MLA_HEADS):
        r0 = hd * LANE
        qtm_ref[0, r0:r0 + 64, :] = (qa[r0:r0 + 64] * c_mla).astype(BF16)
        rope = qa[r0 + 64:r0 + 96] * cos_t + qb[hd * 32:hd * 32 + 32] * sin_t
        qtm_ref[0, r0 + 64:r0 + 96, :] = (rope * c_mla).astype(BF16)
        qtm_ref[0, r0 + 96:r0 + 128, :] = jnp.zeros((32, rope.shape[1]), BF16)

    qn = [jnp.sum((qd * qd).reshape(DIFF_MAPS, DIFF_QK, ts), axis=1)]
    qm = (qa * qa).reshape(MLA_HEADS, LANE, ts)
    qn.append(jnp.sum(qm, axis=1) * (c_mla * c_mla))
    qn.append(jnp.zeros((16 - DIFF_MAPS - MLA_HEADS, ts), F32))
    qmax = jnp.max(jnp.concatenate(qn, axis=0), axis=1, keepdims=True)
    qst_ref[0, 0] = jnp.broadcast_to(qmax, (16, LANE))
    kn = _dot((kd * kd).astype(BF16), gd_ref[...]) + _dot((km * km).astype(BF16), gm_ref[...])
    kst_ref[0, 0] = jnp.broadcast_to(jnp.max(kn, axis=0, keepdims=True), (8, LANE))


def _proj_call(x, g, w, tabs, ts):
    B, S, _ = x.shape
    row = lambda n: pl.BlockSpec((1, ts, n), lambda b, i: (b, i, 0))
    col = lambda n: pl.BlockSpec((1, n, ts), lambda b, i: (b, 0, i))
    full = lambda a: pl.BlockSpec(a.shape, lambda b, i: (0,) * a.ndim)
    consts = [g, w["wnat"], w["wtr"], w["gq"], w["gkv"], w["wqa"], w["wqb"], w["wk"], w["wvt"], w["place"],
              w["group_d"], w["group_m"]]
    n_vt = DIFF_HEADS * VT_ROWS
    nt = S // ts
    stat = lambda r: pl.BlockSpec((1, 1, r, LANE), lambda b, i: (b, i, 0, 0))
    out_shape = [
        jax.ShapeDtypeStruct((B, 256, S), BF16), jax.ShapeDtypeStruct((B, S, 512), BF16),
        jax.ShapeDtypeStruct((B, n_vt, S), BF16), jax.ShapeDtypeStruct((B, 512, S), BF16),
        jax.ShapeDtypeStruct((B, S, 512), BF16), jax.ShapeDtypeStruct((B, n_vt, S), BF16),
        jax.ShapeDtypeStruct((B, 512, S), BF16), jax.ShapeDtypeStruct((B, S, 128), BF16),
        jax.ShapeDtypeStruct((B, 128, S), BF16),
        jax.ShapeDtypeStruct((B, nt, 16, LANE), F32), jax.ShapeDtypeStruct((B, nt, 8, LANE), F32),
    ]
    out_specs = [col(256), row(512), col(n_vt), col(512), row(512), col(n_vt), col(512), row(128), col(128),
                 stat(16), stat(8)]
    return pl.pallas_call(
        _proj_kernel,
        grid=(B, S // ts),
        in_specs=[row(D_MODEL)] + [full(a) for a in consts] + [row(128), col(32), col(32)],
        out_specs=out_specs,
        out_shape=out_shape,
        compiler_params=_params("parallel", "parallel"),
        name="in_proj",
    )(x, *consts, tabs["tab"], tabs["cos_t"], tabs["sin_t"])


STRIP = 256
SLOW_DEPTH = 3
FAST_DEPTH = 2
FAST_KEYS = 512
N_MAPS = DIFF_MAPS + MLA_HEADS


def _split3(x):
    hi = x.astype(BF16).astype(F32)
    mid = (x - hi).astype(BF16).astype(F32)
    lo = (x - hi - mid).astype(BF16).astype(F32)
    return hi, mid, lo


def _dense_kernel(fast_ref, krange_ref, qtd_ref, kd_ref, vtd_ref, qtm_ref, km_ref, vtm_ref, posq_ref, posk_ref,
                  lam_ref, hg_ref, linit_ref, od_ref, om_ref,
                  wq_ref, m_ref, alpha_ref, acc_ref, s_ref):
    b, qi, ki = pl.program_id(0), pl.program_id(1), pl.program_id(2)
    tq = qtd_ref.shape[2]
    n_slots = s_ref.shape[0]

    @pl.when(ki == 0)
    def _init():
        m_ref[...] = jnp.full(m_ref.shape, NEG_BIG, F32)
        acc_ref[...] = jnp.zeros(acc_ref.shape, F32)
        for mp in range(DIFF_MAPS):
            q = qtd_ref[0, mp * DIFF_QK:(mp + 1) * DIFF_QK, :]
            above = jnp.zeros(((mp % 2) * DIFF_QK, tq), BF16)
            below = jnp.zeros((LANE - (mp % 2 + 1) * DIFF_QK, tq), BF16)
            wq_ref[mp] = jnp.concatenate([above, q, below] if mp % 2 else [q, below], axis=0)
        for hd in range(MLA_HEADS):
            wq_ref[DIFF_MAPS + hd] = qtm_ref[0, hd * LANE:(hd + 1) * LANE, :]

    def key_slab(mp, keys):
        if mp < DIFF_MAPS:
            return kd_ref[0, keys, (mp // 2) * LANE:(mp // 2 + 1) * LANE]
        hd = mp - DIFF_MAPS
        return km_ref[0, keys, hd * LANE:(hd + 1) * LANE]

    def values_t(mp, keys):
        if mp < DIFF_MAPS:
            return vtd_ref[0, (mp // 2) * VT_ROWS:(mp // 2 + 1) * VT_ROWS, keys]
        hd = mp - DIFF_MAPS
        return vtm_ref[0, hd * VT_ROWS:(hd + 1) * VT_ROWS, keys]

    def set_shift_rows(shift):
        rows = lax.broadcasted_iota(jnp.int32, (ONES_ROWS, tq), 0)
        for mp in range(N_MAPS):
            hi, mid, lo = _split3(shift[mp:mp + 1, :])
            block = jnp.where(rows == 0, hi, jnp.where(rows == 1, mid, jnp.where(rows == 2, lo, 0.0)))
            first = DIFF_AUG if mp < DIFF_MAPS else MLA_AUG
            wq_ref[mp, first:first + ONES_ROWS, :] = block.astype(BF16)

    def work_items(chunk):
        items = []
        for st in range(tq // STRIP):
            lanes = slice(st * STRIP, (st + 1) * STRIP)
            for kc in range(posk_ref.shape[1] // chunk):
                keys = slice(kc * chunk, (kc + 1) * chunk)
                dist = jnp.abs(posk_ref[0, keys] - posq_ref[0, :, lanes]).astype(F32)
                bias = [dist * (slope * LOG2E) for slope in DIFF_SLOPES]
                items += [(mp, lanes, keys, bias) for mp in range(N_MAPS)]
        return items

    def logits(mp, lanes, keys, bias):
        s = _dot(key_slab(mp, keys), wq_ref[mp, :, lanes])
        return s - bias[mp // 2] if mp < DIFF_MAPS else s

    fast = fast_ref[b, qi, ki] != 0

    @pl.when(fast)
    def _fast_step():
        m_old = m_ref[...]
        kmin, kmax = krange_ref[b, ki, 0], krange_ref[b, ki, 1]
        posq = posq_ref[0]
        outside = jnp.maximum(jnp.maximum(kmin - posq, posq - kmax), 0)
        dmin = (outside + (kmax - kmin)).astype(F32)
        rows = lax.broadcasted_iota(jnp.int32, m_old.shape, 0)
        slope = jnp.zeros(m_old.shape, F32)
        for hd in range(DIFF_HEADS):
            slope = jnp.where(rows // 2 == hd, DIFF_SLOPES[hd] * LOG2E, slope)
        m_eff = jnp.maximum(m_old, -LOGIT_LIMIT - slope * dmin)
        alpha_ref[...] = jnp.exp2(m_old - m_eff)
        m_ref[...] = m_eff
        set_shift_rows(-m_eff)
        items = work_items(FAST_KEYS)
        vals = {}
        for t in range(len(items) + FAST_DEPTH):
            if t < len(items):
                vals[t] = logits(*items[t])
            if t >= FAST_DEPTH:
                mp, lanes, keys, _ = items[t - FAST_DEPTH]
                p = jnp.exp2(vals.pop(t - FAST_DEPTH)).astype(BF16)
                old = acc_ref[mp, :, lanes]
                if keys.start == 0:
                    old = alpha_ref[mp:mp + 1, lanes] * old
                acc_ref[mp, :, lanes] = old + _dot(values_t(mp, keys), p)

    @pl.when(jnp.logical_not(fast))
    def _slow_step():
        set_shift_rows(jnp.zeros(m_ref.shape, F32))
        items = work_items(posk_ref.shape[1])

        def stage_qk(j):
            s_ref[j % n_slots] = logits(*items[j])

        def stage_max(j):
            mp, lanes, _, _ = items[j]
            m_old = m_ref[mp:mp + 1, lanes]
            m_new = jnp.maximum(m_old, jnp.max(s_ref[j % n_slots], axis=0, keepdims=True))
            m_ref[mp:mp + 1, lanes] = m_new
            return m_new, jnp.exp2(m_old - m_new)

        def stage_pv(j, m_new, alpha):
            mp, lanes, keys, _ = items[j]
            p = jnp.exp2((s_ref[j % n_slots] - m_new).astype(BF16))
            acc_ref[mp, :, lanes] = alpha * acc_ref[mp, :, lanes] + _dot(values_t(mp, keys), p)

        stats = {}
        for t in range(len(items) + SLOW_DEPTH):
            if t < len(items):
                stage_qk(t)
            if 1 <= t <= len(items):
                stats[t - 1] = stage_max(t - 1)
            if t >= SLOW_DEPTH:
                stage_pv(t - SLOW_DEPTH, *stats.pop(t - SLOW_DEPTH))

    @pl.when(ki == pl.num_programs(2) - 1)
    def _finish():
        lp = lam_ref[...]
        lam_init = linit_ref[...]
        lam = (jnp.exp(jnp.sum(lp[0:1] * lp[1:2], axis=-1, keepdims=True))
               - jnp.exp(jnp.sum(lp[2:3] * lp[3:4], axis=-1, keepdims=True)) + lam_init)
        def normalised(idx):
            return acc_ref[idx, 0:64, :] / acc_ref[idx, 64:65, :]

        outs = []
        for hd in range(DIFF_HEADS):
            o = normalised(2 * hd) - lam * normalised(2 * hd + 1)
            o = o * lax.rsqrt(jnp.mean(o * o, axis=0, keepdims=True) + EPS) * hg_ref[...]
            outs.append(o * (1.0 - lam_init))
        od_ref[0] = jnp.concatenate(outs, axis=0).T.astype(BF16)
        outs = [normalised(DIFF_MAPS + hd) for hd in range(MLA_HEADS)]
        om_ref[0] = jnp.concatenate(outs, axis=0).T.astype(BF16)


def _dense_call(p, fast, krange, posq, posk, lam, hg, linit, tq, tk):
    B, _, S = p["qtd"].shape
    qcol = lambda n: pl.BlockSpec((1, n, tq), lambda b, i, k, *_: (b, 0, i))
    krow = lambda n: pl.BlockSpec((1, tk, n), lambda b, i, k, *_: (b, k, 0))
    kcol = lambda n: pl.BlockSpec((1, n, tk), lambda b, i, k, *_: (b, 0, k))
    full = lambda a: pl.BlockSpec(a.shape, lambda b, i, k, *_: (0,) * a.ndim)
    n_vt = DIFF_HEADS * VT_ROWS
    grid_spec = pltpu.PrefetchScalarGridSpec(
        num_scalar_prefetch=2,
        grid=(B, S // tq, S // tk),
        in_specs=[qcol(256), krow(512), kcol(n_vt), qcol(512), krow(512), kcol(n_vt),
                  qcol(1), krow(1), full(lam), full(hg), full(linit)],
        out_specs=[pl.BlockSpec((1, tq, 256), lambda b, i, k, *_: (b, i, 0))] * 2,
        scratch_shapes=[pltpu.VMEM((N_MAPS, LANE, tq), BF16),
                        pltpu.VMEM((16, tq), F32), pltpu.VMEM((16, tq), F32),
                        pltpu.VMEM((N_MAPS, VT_ROWS, tq), F32),
                        pltpu.VMEM((SLOW_DEPTH + 2, tk, STRIP), F32)],
    )
    return pl.pallas_call(
        _dense_kernel,
        grid_spec=grid_spec,
        out_shape=[jax.ShapeDtypeStruct((B, S, 256), BF16)] * 2,
        compiler_params=_params("parallel", "parallel", "arbitrary"),
        name="dense_attn",
    )(fast, krange, p["qtd"], p["kd"], p["vtd"], p["qtm"], p["km"], p["vtm"], posq, posk, lam, hg, linit)


SWA_DEPTH = 5


def _swa_kernel(qt_ref, kc_ref, kp_ref, kn_ref, vc_ref, vp_ref, vn_ref,
                posq_ref, pkc_ref, pkp_ref, pkn_ref, sink_ref, o_ref):
    i = pl.program_id(1)
    last = pl.num_programs(1) - 1
    tq = qt_ref.shape[2]
    w = SWA_WINDOW
    n_ext = tq + 2 * w
    k_ext = jnp.concatenate([kp_ref[0], kc_ref[0], kn_ref[0]], axis=0)
    vt_ext = jnp.concatenate([vp_ref[0], vc_ref[0], vn_ref[0]], axis=1)
    pos_ext = jnp.concatenate([pkp_ref[0], pkc_ref[0], pkn_ref[0]], axis=0)
    row = lax.broadcasted_iota(jnp.int32, (n_ext, 1), 0)
    far = 4 * n_ext
    kidx = (row - w + jnp.where(row < w, jnp.where(i == 0, far, 0), 0)
            + jnp.where(row >= tq + w, jnp.where(i == last, far, 0), 0))
    zeros = jnp.zeros((HEAD_DIM, tq), BF16)
    wqs = []
    for hq in range(SWA_Q_HEADS):
        q_h = qt_ref[0, hq * HEAD_DIM:(hq + 1) * HEAD_DIM, :]
        wqs.append(jnp.concatenate([q_h, zeros] if hq // SWA_GROUP == 0 else [zeros, q_h], axis=0))

    def item_logits(r, hq, dist, ok):
        sink = sink_ref[hq:hq + 1, :] * LOG2E
        s = _dot(k_ext[r * w:(r + 3) * w], wqs[hq][:, r * w:(r + 1) * w]) - dist * (SWA_SLOPES[hq] * LOG2E)
        s = jnp.where(ok, s, NEG_BIG)
        m = jnp.maximum(sink, jnp.max(s, axis=0, keepdims=True))
        return s, m, sink

    def item_output(r, hq, s, m, sink):
        kv = hq // SWA_GROUP
        e = jnp.exp2(s - m)
        den = jnp.exp2(sink - m) + jnp.sum(e, axis=0, keepdims=True)
        return _dot(vt_ext[kv * HEAD_DIM:(kv + 1) * HEAD_DIM, r * w:(r + 3) * w], e.astype(BF16)) / den

    items = []
    for r in range(tq // w):
        dist = jnp.abs(pos_ext[r * w:(r + 3) * w] - posq_ref[0, :, r * w:(r + 1) * w]).astype(F32)
        qidx = lax.broadcasted_iota(jnp.int32, (1, w), 1) + r * w
        ok = jnp.abs(kidx[r * w:(r + 3) * w] - qidx) <= w
        items += [(r, hq, dist, ok) for hq in range(SWA_Q_HEADS)]
    outs = {}
    ahead = {}
    for t in range(len(items) + SWA_DEPTH):
        if t < len(items):
            ahead[t] = item_logits(*items[t])
        if t >= SWA_DEPTH:
            r, hq, _, _ = items[t - SWA_DEPTH]
            outs[(hq, r)] = item_output(r, hq, *ahead.pop(t - SWA_DEPTH))
    heads = [jnp.concatenate([outs[(hq, r)] for r in range(tq // w)], axis=1) for hq in range(SWA_Q_HEADS)]
    o_ref[0] = jnp.concatenate(heads, axis=0).T.astype(BF16)


def _swa_call(p, posq, posk, sinks, tq):
    B, _, S = p["qts"].shape
    w = SWA_WINDOW
    r = tq // w
    nblk = S // w
    prev = lambda b, i: (b, jnp.maximum(i * r - 1, 0), 0)
    nxt = lambda b, i: (b, jnp.minimum((i + 1) * r, nblk - 1), 0)
    prev_t = lambda b, i: (b, 0, jnp.maximum(i * r - 1, 0))
    nxt_t = lambda b, i: (b, 0, jnp.minimum((i + 1) * r, nblk - 1))
    cur = lambda b, i: (b, i, 0)
    cur_t = lambda b, i: (b, 0, i)
    return pl.pallas_call(
        _swa_kernel,
        grid=(B, S // tq),
        in_specs=[pl.BlockSpec((1, 512, tq), cur_t),
                  pl.BlockSpec((1, tq, 128), cur), pl.BlockSpec((1, w, 128), prev), pl.BlockSpec((1, w, 128), nxt),
                  pl.BlockSpec((1, 128, tq), cur_t), pl.BlockSpec((1, 128, w), prev_t),
                  pl.BlockSpec((1, 128, w), nxt_t),
                  pl.BlockSpec((1, 1, tq), cur_t),
                  pl.BlockSpec((1, tq, 1), cur), pl.BlockSpec((1, w, 1), prev), pl.BlockSpec((1, w, 1), nxt),
                  pl.BlockSpec(sinks.shape, lambda b, i: (0, 0))],
        out_specs=pl.BlockSpec((1, tq, 512), cur),
        out_shape=jax.ShapeDtypeStruct((B, S, 512), BF16),
        compiler_params=_params("parallel", "parallel"),
        name="swa_attn",
    )(p["qts"], p["ks"], p["ks"], p["ks"], p["vts"], p["vts"], p["vts"],
      posq, posk, posk, posk, sinks)


def _memkv_kernel(mem_ref, g_ref, w_ref, kv_ref):
    mem_n = _rms_rows(mem_ref[0], g_ref[0]).astype(BF16)
    kv_ref[0, 0] = _dot(mem_n, w_ref[0]).astype(BF16)


def _memkv_call(mem, g, w):
    B, M, _ = mem.shape
    depth = w.shape[0]
    return pl.pallas_call(
        _memkv_kernel,
        grid=(depth, B),
        in_specs=[pl.BlockSpec((1, M, D_MODEL), lambda l, b: (b, 0, 0)),
                  pl.BlockSpec((1, 1, D_MODEL), lambda l, b: (l, 0, 0)),
                  pl.BlockSpec((1, D_MODEL, 2 * D_MODEL), lambda l, b: (l, 0, 0))],
        out_specs=pl.BlockSpec((1, 1, M, 2 * D_MODEL), lambda l, b: (l, b, 0, 0)),
        out_shape=jax.ShapeDtypeStruct((depth, B, M, 2 * D_MODEL), BF16),
        compiler_params=_params("parallel", "parallel"),
        name="mem_kv",
    )(mem, g, w)


def _tail_kernel(x_ref, od_ref, om_ref, os_ref, wout_ref, gmp_ref, gxp_ref, wxq_ref, kv_ref, wxo_ref, gxo_ref,
                 gfi_ref, wfi_ref, wfo_ref, gfo_ref, o_ref):
    mix = (_dot(od_ref[0], wout_ref[0:256, :]) + _dot(om_ref[0], wout_ref[256:512, :])
           + _dot(os_ref[0], wout_ref[512:1024, :]))
    x1 = x_ref[0] + _rms_rows(mix, gmp_ref[...])
    hq = _rms_rows(x1, gxp_ref[...]).astype(BF16)
    q = (_dot(hq, wxq_ref[...]) * (LOG2E / math.sqrt(X_HEAD_DIM))).astype(BF16)
    heads = []
    for hd in range(X_HEADS):
        c0 = hd * X_HEAD_DIM
        s = _nt_dot(q[:, c0:c0 + X_HEAD_DIM], kv_ref[0, :, c0:c0 + X_HEAD_DIM])
        e = jnp.exp2(s - jnp.max(s, axis=-1, keepdims=True))
        den = jnp.sum(e, axis=-1, keepdims=True)
        o = _dot(e.astype(BF16), kv_ref[0, :, D_MODEL + c0:D_MODEL + c0 + X_HEAD_DIM])
        heads.append((o / den).astype(BF16))
    xo = _dot(jnp.concatenate(heads, axis=-1), wxo_ref[...])
    x2 = x1 + _rms_rows(xo, gxo_ref[...])
    h = _rms_rows(x2, gfi_ref[...]).astype(BF16)
    gate = _dot(h, wfi_ref[:, 0:D_FF])
    up = _dot(h, wfi_ref[:, D_FF:2 * D_FF])
    f = _dot((gate * jax.nn.sigmoid(gate) * up).astype(BF16), wfo_ref[...])
    o_ref[0] = x2 + _rms_rows(f, gfo_ref[...])


def _tail_call(x, od, om, os_, kv, consts, ts):
    B, S, _ = x.shape
    wout, gmp, gxp, wxq, wxo, gxo, gfi, wfi, wfo, gfo = consts
    row = lambda n: pl.BlockSpec((1, ts, n), lambda b, i: (b, i, 0))
    full = lambda a: pl.BlockSpec(a.shape, lambda b, i: (0,) * a.ndim, pipeline_mode=pl.Buffered(1))
    kv_spec = pl.BlockSpec((1,) + kv.shape[1:], lambda b, i: (b, 0, 0))
    return pl.pallas_call(
        _tail_kernel,
        grid=(B, S // ts),
        in_specs=[row(D_MODEL), row(256), row(256), row(512), full(wout), full(gmp), full(gxp), full(wxq),
                  kv_spec, full(wxo), full(gxo), full(gfi), full(wfi), full(wfo), full(gfo)],
        out_specs=row(D_MODEL),
        out_shape=jax.ShapeDtypeStruct(x.shape, F32),
        compiler_params=_params("parallel", "parallel"),
        name="layer_tail",
    )(x, od, om, os_, wout, gmp, gxp, wxq, kv, wxo, gxo, gfi, wfi, wfo, gfo)


def _prep_weights(w_in, mla_q_norm_g, mla_w_q_up, mla_kv_norm_g, mla_w_kv_up):
    swap = np.concatenate([np.arange(16, 32), np.arange(0, 16)])
    kr = w_in[:, O_BKR:O_CQ]
    zeros64 = jnp.zeros((D_MODEL, 64), w_in.dtype)
    ak = w_in[:, O_AK:O_AV]
    ak_slabs = [a for hd in range(DIFF_HEADS) for a in (ak[:, 64 * hd:64 * (hd + 1)], zeros64)]
    wnat = jnp.concatenate(
        ak_slabs + [w_in[:, O_BCQ:O_BCKV], w_in[:, O_BCKV:O_BKR], kr, kr[:, swap], zeros64,
                    w_in[:, O_CK:O_CV]], axis=1)
    wtr = jnp.concatenate(
        [w_in[:, O_AQ:O_AK], w_in[:, O_AV:O_BCQ], w_in[:, O_CQ:O_CK], w_in[:, O_CV:O_END]], axis=1).T
    qup = mla_w_q_up.reshape(MLA_Q_RANK, MLA_HEADS, MLA_NOPE + MLA_ROPE)
    wqa = jnp.concatenate([qup, jnp.zeros((MLA_Q_RANK, MLA_HEADS, 32), qup.dtype)], axis=2)
    wqa = wqa.reshape(MLA_Q_RANK, MLA_HEADS * LANE).T
    wqb = qup[:, :, MLA_NOPE:][:, :, swap].reshape(MLA_Q_RANK, MLA_HEADS * MLA_ROPE).T
    kvup = mla_w_kv_up.reshape(MLA_KV_RANK, MLA_HEADS, MLA_NOPE + MLA_V)
    wk = jnp.concatenate([kvup[:, :, :MLA_NOPE], jnp.zeros((MLA_KV_RANK, MLA_HEADS, 64), kvup.dtype)], axis=2)
    wk = wk.reshape(MLA_KV_RANK, MLA_HEADS * LANE)
    wvt = kvup[:, :, MLA_NOPE:].reshape(MLA_KV_RANK, MLA_HEADS * MLA_V).T
    place = np.zeros((LANE, MLA_HEADS * LANE), np.float32)
    group_d = np.zeros((DIFF_HEADS * LANE, LANE), np.float32)
    group_m = np.zeros((MLA_HEADS * LANE, LANE), np.float32)
    for hd in range(MLA_HEADS):
        place[np.arange(32), hd * LANE + MLA_NOPE + np.arange(32)] = 1.0
        group_m[hd * LANE + np.arange(MLA_NOPE + MLA_ROPE), DIFF_MAPS + hd] = 1.0
    for mp in range(DIFF_MAPS):
        group_d[(mp // 2) * LANE + (mp % 2) * DIFF_QK + np.arange(DIFF_QK), mp] = 1.0
    return {
        "group_d": jnp.asarray(group_d, BF16), "group_m": jnp.asarray(group_m, BF16),
        "wnat": wnat.astype(BF16), "wtr": wtr.astype(BF16),
        "gq": mla_q_norm_g.reshape(1, -1), "gkv": mla_kv_norm_g.reshape(1, -1),
        "wqa": wqa.astype(BF16), "wqb": wqb.astype(BF16), "wk": wk.astype(BF16), "wvt": wvt.astype(BF16),
        "place": jnp.asarray(place, BF16),
    }


def _rope_tables(positions):
    half = MLA_ROPE // 2
    inv = ROPE_THETA ** (-jnp.arange(half, dtype=F32) / half)
    ang = positions.astype(F32)[..., None] * inv
    cos, sin = jnp.cos(ang), jnp.sin(ang)
    cos2 = jnp.concatenate([cos, cos], axis=-1)
    sin2 = jnp.concatenate([-sin, sin], axis=-1)
    tab = jnp.concatenate([cos2, sin2, jnp.zeros(cos2.shape[:2] + (64,), F32)], axis=-1)
    return {"tab": tab, "cos_t": cos2.swapaxes(1, 2), "sin_t": sin2.swapaxes(1, 2)}


def _key_ranges(positions, nk):
    ktile = positions.reshape(positions.shape[0], nk, -1)
    return jnp.stack([ktile.min(axis=-1), ktile.max(axis=-1)], axis=-1)


def _fast_tiles(qstat, kstat, krange, nq, nk):
    B, nt = qstat.shape[:2]
    assert nt % nq == 0 and nt % nk == 0, (nt, nq, nk)
    qn = qstat[:, :, :N_MAPS, 0].reshape(B, nq, -1, N_MAPS).max(axis=2)
    kn = kstat[:, :, 0, :N_MAPS].reshape(B, nk, -1, N_MAPS).max(axis=2)
    bound2 = qn[:, :, None, :] * kn[:, None, :, :]
    margin = 0.97
    bounded = jnp.all(bound2 <= (margin * LOGIT_LIMIT) ** 2, axis=-1)
    spread = (krange[:, :, 1] - krange[:, :, 0]).astype(F32) * (max(DIFF_SLOPES) * LOG2E)
    return (bounded & (spread <= SPREAD_LIMIT)[:, None, :]).astype(jnp.int32)


def _tile(n, want):
    t = min(n, want)
    assert n % t == 0, (n, t)
    return t


def kernel(x, mem, positions, g_mix_pre, g_mix_post, w_in, diff_lambda, diff_head_g, mla_q_norm_g, mla_w_q_up,
           mla_kv_norm_g, mla_w_kv_up, swa_sinks, w_out, g_x_pre, g_x_mem, g_x_post, w_xq, w_xkv, w_xo,
           g_ffn_pre, g_ffn_post, w_ffn_in, w_ffn_out):
    B, S, _ = x.shape
    depth = w_in.shape[0]
    ts_proj, ts_tail = _tile(S, 512), _tile(S, 256)
    tq, tk, tq_swa = _tile(S, 2048), _tile(S, 512), _tile(S, 512)

    tabs = _rope_tables(positions)
    posq = positions.reshape(B, 1, S)
    posk = positions.reshape(B, S, 1)
    krange = _key_ranges(positions, S // tk)
    row = lambda v: v.reshape(1, -1)
    kv_all = _memkv_call(mem, g_x_mem.reshape(depth, 1, D_MODEL), w_xkv.astype(BF16))

    for l in range(depth):
        w = _prep_weights(w_in[l].astype(BF16), mla_q_norm_g[l], mla_w_q_up[l].astype(BF16),
                          mla_kv_norm_g[l], mla_w_kv_up[l].astype(BF16))
        names = ("qtd", "kd", "vtd", "qtm", "km", "vtm", "qts", "ks", "vts", "qstat", "kstat")
        p = dict(zip(names, _proj_call(x, row(g_mix_pre[l]), w, tabs, ts_proj)))
        linit = jnp.full((1, 1), 0.8 - 0.6 * math.exp(-0.3 * l), F32)
        fast = _fast_tiles(p["qstat"], p["kstat"], krange, S // tq, S // tk)
        od, om = _dense_call(p, fast, krange, posq, posk, diff_lambda[l], diff_head_g[l].reshape(-1, 1), linit, tq, tk)
        os_ = _swa_call(p, posq, posk, swa_sinks[l].reshape(-1, 1), tq_swa)
        kv = kv_all[l]
        consts = (w_out[l].astype(BF16), row(g_mix_post[l]), row(g_x_pre[l]), w_xq[l].astype(BF16),
                  w_xo[l].astype(BF16), row(g_x_post[l]), row(g_ffn_pre[l]), w_ffn_in[l].astype(BF16),
                  w_ffn_out[l].astype(BF16), row(g_ffn_post[l]))
        x = _tail_call(x, od, om, os_, kv, consts, ts_tail)
    return x
```
